```python
import math
import jax, jax.numpy as jnp
from jax import lax
import numpy as np

D_MODEL = 1024
BATCH = 8
SEQ = 2048
DEPTH = 4
DEC_BATCH = 128
DEC_SEQ = 8
PAST_LEN = 16384
PAGE_SIZE = 128

HEAD_DIM = D_MODEL // 16
M_HEADS = 6
G_HEADS = 6
H_HEADS = 4
M_W = M_HEADS * HEAD_DIM
G_W = G_HEADS * HEAD_DIM
H_W = H_HEADS * HEAD_DIM
D_MIX = M_W + G_W + H_W
CONV_W = 4
D_FF = 4 * D_MODEL
CHUNK = 64
EPS = 1e-6
SPLITS = (M_W, M_W, M_W, M_W, M_HEADS, M_HEADS,
          3 * G_W, G_W, G_HEADS, G_HEADS,
          H_W, H_W, H_W, H_W)
P_IN = 4 * M_W + 2 * M_HEADS + 4 * G_W + 2 * G_HEADS + 4 * H_W

kernel_name = 'hybrid_mlstm_gdn_hgrn2_step'

F32 = jnp.float32


def rmsnorm(x, w):
    xf = x.astype(F32)
    y = xf * lax.rsqrt(jnp.mean(xf * xf, axis=-1, keepdims=True) + EPS)
    return (y * w.astype(F32)).astype(x.dtype)


def head_rmsnorm(h, w):
    y = h * lax.rsqrt(jnp.mean(h * h, axis=-1, keepdims=True) + EPS)
    return y.reshape(h.shape[:2] + (-1,)) * w.astype(F32)


def l2norm(x):
    return x * lax.rsqrt(jnp.sum(x * x, axis=-1, keepdims=True) + EPS)


def heads(a, n):
    return a.reshape(a.shape[:-1] + (n, a.shape[-1] // n))


def split_cols(z):
    idx = np.cumsum(np.array(SPLITS))[:-1].tolist()
    return jnp.split(z, idx, axis=-1)


def to_chunks(a, c):
    b, l = a.shape[:2]
    return jnp.swapaxes(a.reshape((b, l // c, c) + a.shape[2:]), 0, 1)


def from_chunks(a):
    a = jnp.swapaxes(a, 0, 1)
    return a.reshape((a.shape[0], a.shape[1] * a.shape[2]) + a.shape[3:])


def mlstm_scan(q, k, v, ig, lf, C0, n0, m0):
    L = q.shape[1]
    c = math.gcd(L, CHUNK)
    causal = jnp.tril(jnp.ones((c, c), dtype=bool))

    def step(carry, inp):
        C, n, m = carry
        qc, kc, vc, igc, lfc = inp
        b = jnp.cumsum(lfc, axis=1)
        dmat = b[:, :, None, :] - b[:, None, :, :] + igc[:, None, :, :]
        dmat = jnp.where(causal[None, :, :, None], dmat, -jnp.inf)
        inter = b + m[:, None, :]
        m_t = jnp.maximum(inter, jnp.max(dmat, axis=2))
        w = jnp.exp(dmat - m_t[:, :, None, :])
        wi = jnp.exp(inter - m_t)
        wqk = w * jnp.einsum('bthd,bshd->btsh', qc, kc)
        num = jnp.einsum('btsh,bshd->bthd', wqk, vc) + wi[..., None] * jnp.einsum('bthk,bhkv->bthv', qc, C)
        den = jnp.sum(wqk, axis=2) + wi * jnp.einsum('bthk,bhk->bth', qc, n)
        h = num / jnp.maximum(jnp.abs(den), 1.0)[..., None]
        m_new = m_t[:, -1]
        ws = jnp.exp(b[:, -1:, :] - b + igc - m_new[:, None, :])
        decay = jnp.exp(b[:, -1] + m - m_new)
        C_new = decay[..., None, None] * C + jnp.einsum('bsh,bshk,bshv->bhkv', ws, kc, vc)
        n_new = decay[..., None] * n + jnp.einsum('bsh,bshk->bhk', ws, kc)
        return (C_new, n_new, m_new), h

    xs = (to_chunks(q, c), to_chunks(k, c), to_chunks(v, c), to_chunks(ig, c), to_chunks(lf, c))
    (C1, n1, m1), h = lax.scan(step, (C0, n0, m0), xs)
    return from_chunks(h), C1, n1, m1


def gdn_scan(q, k, v, beta, lg, S0):
    L = q.shape[1]
    c = math.gcd(L, CHUNK)
    strict = jnp.tril(jnp.ones((c, c), dtype=bool), -1)
    incl = jnp.tril(jnp.ones((c, c), dtype=bool))
    eye = jnp.eye(c, dtype=F32)

    def step(S, inp):
        qc, kc, vc, bc, lgc = inp
        qh = jnp.swapaxes(qc, 1, 2)
        kh = jnp.swapaxes(kc, 1, 2)
        vh = jnp.swapaxes(vc, 1, 2)
        bh = jnp.swapaxes(bc, 1, 2)
        g = jnp.cumsum(jnp.swapaxes(lgc, 1, 2), axis=-1)
        diff = g[..., :, None] - g[..., None, :]
        kb = kh * bh[..., None]
        lmat = jnp.where(strict, jnp.einsum('bhtk,bhsk->bhts', kb, kh) * jnp.exp(jnp.where(strict, diff, 0.0)), 0.0)
        rhs = jnp.concatenate([vh * bh[..., None], kb * jnp.exp(g)[..., None]], axis=-1)
        sol = lax.linalg.triangular_solve(eye + lmat, rhs, left_side=True, lower=True, unit_diagonal=True)
        dv = vh.shape[-1]
        u = sol[..., :dv]
        wk = sol[..., dv:]
        v_new = u - jnp.einsum('bhtk,bhkv->bhtv', wk, S)
        attn = jnp.where(incl, jnp.einsum('bhtk,bhsk->bhts', qh, kh) * jnp.exp(jnp.where(incl, diff, 0.0)), 0.0)
        o = jnp.einsum('bhtk,bhkv->bhtv', qh * jnp.exp(g)[..., None], S) + jnp.einsum('bhts,bhsv->bhtv', attn, v_new)
        gl = g[..., -1]
        S_new = S * jnp.exp(gl)[..., None, None] + jnp.einsum('bhsk,bhsv->bhkv', kh * jnp.exp(gl[..., None] - g)[..., None], v_new)
        return S_new, jnp.swapaxes(o, 1, 2)

    xs = (to_chunks(q, c), to_chunks(k, c), to_chunks(v, c), to_chunks(beta, c), to_chunks(lg, c))
    S1, o = lax.scan(step, S0, xs)
    return from_chunks(o), S1


def hgrn_scan(q, k, v, lf, S0):
    L = q.shape[1]
    c = math.gcd(L, CHUNK)
    incl = jnp.tril(jnp.ones((c, c), dtype=bool))

    def step(S, inp):
        qc, kc, vc, lfc = inp
        G = jnp.cumsum(lfc, axis=1)
        diff = jnp.where(incl[None, :, :, None, None], G[:, :, None] - G[:, None, :], -jnp.inf)
        attn = jnp.einsum('bthk,bshk,btshk->bhts', qc, kc, jnp.exp(diff))
        o = jnp.einsum('bthk,bhkv->bthv', qc * jnp.exp(G), S) + jnp.einsum('bhts,bshv->bthv', attn, vc)
        Gl = G[:, -1]
        S_new = jnp.exp(Gl)[..., None] * S + jnp.einsum('bshk,bshv->bhkv', kc * jnp.exp(Gl[:, None] - G), vc)
        return S_new, o

    xs = (to_chunks(q, c), to_chunks(k, c), to_chunks(v, c), to_chunks(lf, c))
    S1, o = lax.scan(step, S0, xs)
    return from_chunks(o), S1


def mixer(h, w_in, b_in, f_bias, conv_w, A_log, dt_bias, lb, out_norm, w_out, st):
    C0, n0, m0, Sg0, buf0, Sh0 = st
    L = h.shape[1]
    z = (h @ w_in + b_in).astype(F32)
    mq, mk, mv, mo, mi, mf, gqkv, gg, gb, ga, hq, hf, hi, hg = split_cols(z)
    hm, C1, n1, m1 = mlstm_scan(heads(mq, M_HEADS), heads(mk, M_HEADS) * HEAD_DIM ** -0.5, heads(mv, M_HEADS),
                                mi, jax.nn.log_sigmoid(mf + f_bias.astype(F32)),
                                C0.astype(F32), n0.astype(F32), m0.astype(F32))
    ym = head_rmsnorm(hm, out_norm[:M_W]) * jax.nn.sigmoid(mo)
    xc = jnp.concatenate([buf0.astype(F32), gqkv], axis=1)
    cw = conv_w.astype(F32)
    conv = jax.nn.silu(sum(cw[j] * xc[:, j:j + L] for j in range(CONV_W)))
    buf1 = xc[:, L:]
    gq, gk, gv = jnp.split(conv, 3, axis=-1)
    gq = l2norm(heads(gq, G_HEADS)) * HEAD_DIM ** -0.5
    gk = l2norm(heads(gk, G_HEADS))
    beta = jax.nn.sigmoid(gb)
    lg = -jnp.exp(A_log.astype(F32)) * jax.nn.softplus(ga + dt_bias.astype(F32))
    hgd, Sg1 = gdn_scan(gq, gk, heads(gv, G_HEADS), beta, lg, Sg0.astype(F32))
    yg = head_rmsnorm(hgd, out_norm[M_W:M_W + G_W]) * jax.nn.silu(gg)
    lf = jnp.logaddexp(jnp.log(lb), jnp.log1p(-lb) + jax.nn.log_sigmoid(hf))
    kk = (1.0 - lb) * jax.nn.sigmoid(-hf)
    hh, Sh1 = hgrn_scan(heads(hq, H_HEADS), heads(kk, H_HEADS), heads(hi, H_HEADS), heads(lf, H_HEADS), Sh0.astype(F32))
    yh = head_rmsnorm(hh, out_norm[M_W + G_W:]) * jax.nn.silu(hg)
    y = jnp.concatenate([ym, yg, yh], axis=-1).astype(h.dtype) @ w_out
    return y, (C1, n1, m1, Sg1, buf1, Sh1)


def trunk(x, c, states, params):
    (w_ada, b_ada, norm_pre_mix, norm_post_mix, w_in, b_in, mlstm_f_bias, gdn_conv_w, gdn_A_log,
     gdn_dt_bias, hgrn_lb_logits, mix_out_norm, w_out, norm_pre_ffn, norm_post_ffn, w_ff1, w_ff2) = params
    lb_all = jnp.cumsum(jax.nn.softmax(hgrn_lb_logits.astype(F32), axis=0), axis=0)
    lb_all = lb_all - lb_all[0:1]
    new = ([], [], [], [], [], [])
    for l in range(DEPTH):
        st = (states[0][l], states[1][l], states[2][l], states[3][l], states[4][l], states[5][l])
        mod = (c @ w_ada[l] + b_ada[l])[:, None, :]
        sh1, sc1, g1, sh2, sc2, g2 = jnp.split(mod, 6, axis=-1)
        h = rmsnorm(x, norm_pre_mix[l]) * (1 + sc1) + sh1
        y, st1 = mixer(h, w_in[l], b_in[l], mlstm_f_bias[l], gdn_conv_w[l], gdn_A_log[l], gdn_dt_bias[l],
                       lb_all[l], mix_out_norm[l], w_out[l], st)
        x = x + g1 * rmsnorm(y, norm_post_mix[l])
        h = rmsnorm(x, norm_pre_ffn[l]) * (1 + sc2) + sh2
        f = jnp.square(jax.nn.relu(h @ w_ff1[l])) @ w_ff2[l]
        x = x + g2 * rmsnorm(f, norm_post_ffn[l])
        for lst, s in zip(new, st1):
            lst.append(s)
    return x, tuple(jnp.stack(s) for s in new)


def setup_inputs(seed: int = 0) -> dict:
    key = jax.random.key(seed)
    ks = jax.random.split(key, 32)

    def nrm(k, shape, s):
        return jax.random.normal(k, shape, F32) * s

    def gain(k, shape):
        return 1.0 + 0.02 * jax.random.normal(k, shape, F32)

    dt = jnp.exp(jax.random.uniform(ks[19], (DEPTH, G_HEADS), F32, math.log(1e-3), math.log(1e-1)))
    return {
        'x_prompt': nrm(ks[0], (BATCH, SEQ, D_MODEL), 1.0),
        'x_sample': nrm(ks[1], (DEC_BATCH, DEC_SEQ, D_MODEL), 1.0),
        'state_mlstm_C': nrm(ks[2], (DEPTH, DEC_BATCH, M_HEADS, HEAD_DIM, HEAD_DIM), 0.1),
        'state_mlstm_n': nrm(ks[3], (DEPTH, DEC_BATCH, M_HEADS, HEAD_DIM), 0.1),
        'state_mlstm_m': nrm(ks[4], (DEPTH, DEC_BATCH, M_HEADS), 1.0),
        'state_gdn_S': nrm(ks[5], (DEPTH, DEC_BATCH, G_HEADS, HEAD_DIM, HEAD_DIM), 0.1),
        'state_gdn_conv': nrm(ks[6], (DEPTH, DEC_BATCH, CONV_W - 1, 3 * G_W), 1.0),
        'state_hgrn_S': nrm(ks[7], (DEPTH, DEC_BATCH, H_HEADS, HEAD_DIM, HEAD_DIM), 0.5),
        'c_prompt': nrm(ks[8], (BATCH, D_MODEL), 1.0),
        'c_sample': nrm(ks[9], (DEC_BATCH, D_MODEL), 1.0),
        'w_ada': nrm(ks[10], (DEPTH, D_MODEL, 6 * D_MODEL), 0.5 * D_MODEL ** -0.5),
        'b_ada': nrm(ks[11], (DEPTH, 6 * D_MODEL), 0.02),
        'norm_pre_mix': gain(ks[12], (DEPTH, D_MODEL)),
        'norm_post_mix': gain(ks[13], (DEPTH, D_MODEL)),
        'w_in': nrm(ks[14], (DEPTH, D_MODEL, P_IN), D_MODEL ** -0.5),
        'b_in': nrm(ks[15], (DEPTH, P_IN), 0.02),
        'mlstm_f_bias': jnp.linspace(3.0, 6.0, M_HEADS, dtype=F32)[None, :] + nrm(ks[16], (DEPTH, M_HEADS), 0.1),
        'gdn_conv_w': nrm(ks[17], (DEPTH, CONV_W, 3 * G_W), CONV_W ** -0.5),
        'gdn_A_log': jnp.log(jax.random.uniform(ks[18], (DEPTH, G_HEADS), F32, 1.0, 16.0)),
        'gdn_dt_bias': dt + jnp.log(-jnp.expm1(-dt)),
        'hgrn_lb_logits': nrm(ks[20], (DEPTH, H_W), 0.1),
        'mix_out_norm': gain(ks[21], (DEPTH, D_MIX)),
        'w_out': nrm(ks[22], (DEPTH, D_MIX, D_MODEL), D_MIX ** -0.5),
        'norm_pre_ffn': gain(ks[23], (DEPTH, D_MODEL)),
        'norm_post_ffn': gain(ks[24], (DEPTH, D_MODEL)),
        'w_ff1': nrm(ks[25], (DEPTH, D_MODEL, D_FF), D_MODEL ** -0.5),
        'w_ff2': nrm(ks[26], (DEPTH, D_FF, D_MODEL), D_FF ** -0.5),
    }


def reference(x_prompt, x_sample, state_mlstm_C, state_mlstm_n, state_mlstm_m, state_gdn_S, state_gdn_conv,
              state_hgrn_S, c_prompt, c_sample, w_ada, b_ada, norm_pre_mix, norm_post_mix, w_in, b_in,
              mlstm_f_bias, gdn_conv_w, gdn_A_log, gdn_dt_bias, hgrn_lb_logits, mix_out_norm, w_out,
              norm_pre_ffn, norm_post_ffn, w_ff1, w_ff2):
    params = (w_ada, b_ada, norm_pre_mix, norm_post_mix, w_in, b_in, mlstm_f_bias, gdn_conv_w, gdn_A_log,
              gdn_dt_bias, hgrn_lb_logits, mix_out_norm, w_out, norm_pre_ffn, norm_post_ffn, w_ff1, w_ff2)
    bp = x_prompt.shape[0]
    init = (jnp.zeros((DEPTH, bp, M_HEADS, HEAD_DIM, HEAD_DIM), F32),
            jnp.zeros((DEPTH, bp, M_HEADS, HEAD_DIM), F32),
            jnp.zeros((DEPTH, bp, M_HEADS), F32),
            jnp.zeros((DEPTH, bp, G_HEADS, HEAD_DIM, HEAD_DIM), F32),
            jnp.zeros((DEPTH, bp, CONV_W - 1, 3 * G_W), F32),
            jnp.zeros((DEPTH, bp, H_HEADS, HEAD_DIM, HEAD_DIM), F32))
    y_prompt, (p_C, p_n, p_m, p_gS, p_conv, p_hS) = trunk(x_prompt, c_prompt, init, params)
    past = (state_mlstm_C, state_mlstm_n, state_mlstm_m, state_gdn_S, state_gdn_conv, state_hgrn_S)
    y_sample, (s_C, s_n, s_m, s_gS, s_conv, s_hS) = trunk(x_sample, c_sample, past, params)
    return (y_prompt, y_sample, p_C, p_n, p_m, p_gS, p_conv, p_hS, s_C, s_n, s_m, s_gS, s_conv, s_hS)
```

```python
import functools

import jax
import jax.numpy as jnp
from jax import lax
from jax.experimental import pallas as pl
from jax.experimental.pallas import tpu as pltpu

F32 = jnp.float32
BF16 = jnp.bfloat16

D_MODEL = 1024
DEPTH = 4
HEAD_DIM = 64
M_HEADS = 6
G_HEADS = 6
H_HEADS = 4
M_W = M_HEADS * HEAD_DIM
G_W = G_HEADS * HEAD_DIM
H_W = H_HEADS * HEAD_DIM
D_MIX = M_W + G_W + H_W
CONV_W = 4
D_FF = 4 * D_MODEL
EPS = 1e-6
QK_SCALE = HEAD_DIM ** -0.5

LANES = 128
SUBLANES = 8
VMEM_LIMIT_BYTES = 56 * 1024 * 1024

OFF_MQ = 0
OFF_MK = OFF_MQ + M_W
OFF_MV = OFF_MK + M_W
OFF_MO = OFF_MV + M_W
OFF_GQKV = OFF_MO + M_W
OFF_GG = OFF_GQKV + 3 * G_W
OFF_HQ = OFF_GG + G_W
OFF_HF = OFF_HQ + H_W
OFF_HI = OFF_HF + H_W
OFF_HG = OFF_HI + H_W
OFF_GATE = OFF_HG + H_W
Z_W = OFF_GATE + LANES
GATE_MI = 0
GATE_MF = GATE_MI + M_HEADS
GATE_GB = GATE_MF + M_HEADS
GATE_GA = GATE_GB + G_HEADS
GATE_END = GATE_GA + G_HEADS

PROMPT_CHUNK = 64
HGRN_SUBBLOCK = 16
IN_TILE = 512
FFN_TILE = 512
FF_BLOCK = 1024
ADA_BLOCK = 1536


def _mm(a, b):
    return jnp.dot(a.astype(BF16), b.astype(BF16), preferred_element_type=F32)


def _mm_nt(a, b):
    return lax.dot_general(a.astype(BF16), b.astype(BF16), (((1,), (1,)), ((), ())), preferred_element_type=F32)


def _mm_tn(a, b):
    return lax.dot_general(a.astype(BF16), b.astype(BF16), (((0,), (0,)), ((), ())), preferred_element_type=F32)


def _mm_f32(a, b):
    return jnp.dot(a, b, preferred_element_type=F32, precision=lax.Precision.HIGHEST)


def _sigmoid(x):
    return 1.0 / (1.0 + jnp.exp(-x))


def _log1pexp_negabs(x):
    return jnp.log1p(jnp.exp(-jnp.abs(x)))


def _rms(x, w):
    ms = jnp.mean(x * x, axis=-1, keepdims=True)
    return x * lax.rsqrt(ms + EPS) * w


def _row_to_col(row, eye):
    return jnp.sum(eye * row, axis=1, keepdims=True)


def _ada_kernel(c_ref, w_ref, b_ref, o_ref):
    o_ref[...] = jnp.dot(c_ref[...].astype(BF16), w_ref[...].astype(BF16), preferred_element_type=F32) + b_ref[...]


def _ada_call(c_all, w_ada, b_ada):
    rows = c_all.shape[0]
    nblk = (6 * D_MODEL) // ADA_BLOCK
    return pl.pallas_call(
        _ada_kernel,
        grid=(DEPTH, nblk),
        in_specs=[
            pl.BlockSpec((rows, D_MODEL), lambda l, j: (0, 0)),
            pl.BlockSpec((None, D_MODEL, ADA_BLOCK), lambda l, j: (l, 0, j)),
            pl.BlockSpec((None, 1, ADA_BLOCK), lambda l, j: (l, 0, j)),
        ],
        out_specs=pl.BlockSpec((None, rows, ADA_BLOCK), lambda l, j: (l, 0, j)),
        out_shape=jax.ShapeDtypeStruct((DEPTH, rows, 6 * D_MODEL), F32),
        compiler_params=pltpu.CompilerParams(
            dimension_semantics=("parallel", "parallel"), vmem_limit_bytes=VMEM_LIMIT_BYTES),
        name="adaln",
    )(c_all, w_ada, b_ada.reshape(DEPTH, 1, 6 * D_MODEL))


def _inproj_kernel(x_ref, mod_ref, npre_ref, w_ref, b_ref, z_ref):
    bb, t, _ = x_ref.shape
    mod = mod_ref[...]
    sh1 = mod[:, :, 0:D_MODEL]
    sc1 = mod[:, :, D_MODEL:2 * D_MODEL]
    h = _rms(x_ref[...], npre_ref[...]) * (1.0 + sc1) + sh1
    hb = h.reshape(bb * t, D_MODEL).astype(BF16)
    z_ref[...] = jnp.dot(hb, w_ref[...], preferred_element_type=F32) + b_ref[...]


def _inproj_call(x, mod, npre, w, b, bb, t):
    nb, seq, _ = x.shape
    grid = (nb // bb, seq // t)
    nseq = seq // t
    return pl.pallas_call(
        _inproj_kernel,
        grid=grid,
        in_specs=[
            pl.BlockSpec((bb, t, D_MODEL), lambda i, j: (i, j, 0)),
            pl.BlockSpec((bb, 1, 6 * D_MODEL), lambda i, j: (i, 0, 0)),
            pl.BlockSpec((1, D_MODEL), lambda i, j: (0, 0)),
            pl.BlockSpec((D_MODEL, Z_W), lambda i, j: (0, 0)),
            pl.BlockSpec((1, Z_W), lambda i, j: (0, 0)),
        ],
        out_specs=pl.BlockSpec((bb * t, Z_W), lambda i, j: (i * nseq + j, 0)),
        out_shape=jax.ShapeDtypeStruct((nb * seq, Z_W), F32),
        compiler_params=pltpu.CompilerParams(
            dimension_semantics=("parallel", "parallel"), vmem_limit_bytes=VMEM_LIMIT_BYTES),
        name="inproj",
    )(x, mod, npre, w, b)


def _ffn_kernel(x_ref, ym_ref, mod_ref, wout_ref, w1_ref, w2_ref, npost_ref, npre_ref, npostf_ref,
                o_ref, x1_s, h2_s, acc_s):
    j = pl.program_id(2)
    bb, t, _ = x_ref.shape

    @pl.when(j == 0)
    def _():
        mod = mod_ref[...]
        g1 = mod[:, :, 2 * D_MODEL:3 * D_MODEL]
        sh2 = mod[:, :, 3 * D_MODEL:4 * D_MODEL]
        sc2 = mod[:, :, 4 * D_MODEL:5 * D_MODEL]
        y = jnp.dot(ym_ref[...].astype(BF16), wout_ref[...], preferred_element_type=F32)
        x1 = x_ref[...] + g1 * _rms(y.reshape(bb, t, D_MODEL), npost_ref[...])
        x1_s[...] = x1
        h2 = _rms(x1, npre_ref[...]) * (1.0 + sc2) + sh2
        h2_s[...] = h2.reshape(bb * t, D_MODEL).astype(BF16)
        acc_s[...] = jnp.zeros_like(acc_s)

    a = jnp.dot(h2_s[...], w1_ref[...], preferred_element_type=F32)
    a = jnp.square(jnp.maximum(a, 0.0))
    acc_s[...] += jnp.dot(a.astype(BF16), w2_ref[...], preferred_element_type=F32)

    @pl.when(j == pl.num_programs(2) - 1)
    def _():
        g2 = mod_ref[...][:, :, 5 * D_MODEL:6 * D_MODEL]
        f = acc_s[...].reshape(bb, t, D_MODEL)
        o_ref[...] = x1_s[...] + g2 * _rms(f, npostf_ref[...])


def _ffn_call(x, ymix, mod, wout, w1, w2, npost, npre, npostf, bb, t):
    nb, seq, _ = x.shape
    nseq = seq // t
    nff = D_FF // FF_BLOCK
    return pl.pallas_call(
        _ffn_kernel,
        grid=(nb // bb, nseq, nff),
        in_specs=[
            pl.BlockSpec((bb, t, D_MODEL), lambda i, s, j: (i, s, 0)),
            pl.BlockSpec((bb * t, D_MIX), lambda i, s, j: (i * nseq + s, 0)),
            pl.BlockSpec((bb, 1, 6 * D_MODEL), lambda i, s, j: (i, 0, 0)),
            pl.BlockSpec((D_MIX, D_MODEL), lambda i, s, j: (0, 0)),
            pl.BlockSpec((D_MODEL, FF_BLOCK), lambda i, s, j: (0, j)),
            pl.BlockSpec((FF_BLOCK, D_MODEL), lambda i, s, j: (j, 0)),
            pl.BlockSpec((1, D_MODEL), lambda i, s, j: (0, 0)),
            pl.BlockSpec((1, D_MODEL), lambda i, s, j: (0, 0)),
            pl.BlockSpec((1, D_MODEL), lambda i, s, j: (0, 0)),
        ],
        out_specs=pl.BlockSpec((bb, t, D_MODEL), lambda i, s, j: (i, s, 0)),
        out_shape=jax.ShapeDtypeStruct(x.shape, F32),
        scratch_shapes=[
            pltpu.VMEM((bb, t, D_MODEL), F32),
            pltpu.VMEM((bb * t, D_MODEL), BF16),
            pltpu.VMEM((bb * t, D_MODEL), F32),
        ],
        compiler_params=pltpu.CompilerParams(
            dimension_semantics=("parallel", "parallel", "arbitrary"), vmem_limit_bytes=VMEM_LIMIT_BYTES),
        name="outffn",
    )(x, ymix, mod, wout, w1, w2, npost, npre, npostf)


def _unit_lower_inverse(lm, eye, row, col, c):
    t = None
    s = 1
    while s < c:
        same = (row // (2 * s)) == (col // (2 * s))
        off = same & ((row % (2 * s)) >= s) & ((col % (2 * s)) < s)
        coff = jnp.where(off, lm, 0.0)
        t = eye - coff if t is None else t - _mm(_mm(t, coff), t)
        s *= 2
    return t


def _mixer_kernel(layer, c, sb,
                  z_ref, prm_ref, cw_ref, lbl_ref, wn_ref, c0_ref, n0_ref, m0_ref, s0_ref, cv0_ref, h0_ref,
                  y_ref, c1_ref, n1_ref, m1_ref, s1_ref, cv1_ref, h1_ref, xbuf):
    step = pl.program_id(1)

    @pl.when(step == 0)
    def _():
        c1_ref[...] = c0_ref[...]
        n1_ref[...] = n0_ref[...]
        m1_ref[...] = m0_ref[...]
        s1_ref[...] = s0_ref[...]
        h1_ref[...] = h0_ref[...]
        xbuf[0:SUBLANES, :] = cv0_ref[0]

    row = lax.broadcasted_iota(jnp.int32, (c, c), 0)
    col = lax.broadcasted_iota(jnp.int32, (c, c), 1)
    incl = row >= col
    strict = row > col
    eye_c = (row == col).astype(F32)
    tri = incl.astype(F32)
    r64 = lax.broadcasted_iota(jnp.int32, (HEAD_DIM, HEAD_DIM), 0)
    c64 = lax.broadcasted_iota(jnp.int32, (HEAD_DIM, HEAD_DIM), 1)
    eye_d = (r64 == c64).astype(F32)

    zg = z_ref[0, :, OFF_GATE:OFF_GATE + LANES]
    lane = lax.broadcasted_iota(jnp.int32, (1, LANES), 1)
    tg = zg + prm_ref[0:1, :]
    tail = _log1pexp_negabs(tg)
    logsig = -(jnp.maximum(-tg, 0.0) + tail)
    splus = jnp.maximum(tg, 0.0) + tail
    lgv = -jnp.exp(prm_ref[1:2, :]) * splus
    gates = jnp.where(lane < GATE_MF, zg,
                      jnp.where(lane < GATE_GB, logsig,
                                jnp.where(lane < GATE_GA, _sigmoid(tg),
                                          jnp.where(lane < GATE_END, lgv, 0.0))))
    cs = _mm_f32(tri, gates)
    gates_t = gates.T
    cs_t = cs.T

    for h in range(M_HEADS):
        lo, hi = h * HEAD_DIM, (h + 1) * HEAD_DIM
        q = z_ref[0, :, OFF_MQ + lo:OFF_MQ + hi]
        k = z_ref[0, :, OFF_MK + lo:OFF_MK + hi] * QK_SCALE
        v = z_ref[0, :, OFF_MV + lo:OFF_MV + hi]
        og = z_ref[0, :, OFF_MO + lo:OFF_MO + hi]
        ig_row = gates_t[GATE_MI + h:GATE_MI + h + 1, :]
        ig_col = gates[:, GATE_MI + h:GATE_MI + h + 1]
        b_col = cs[:, GATE_MF + h:GATE_MF + h + 1]
        b_row = cs_t[GATE_MF + h:GATE_MF + h + 1, :]
        m_prev = m1_ref[0, :, h:h + 1]
        dmat = jnp.where(incl, b_col - b_row + ig_row, -jnp.inf)
        inter = b_col + m_prev
        m_t = jnp.maximum(inter, jnp.max(dmat, axis=1, keepdims=True))
        w = jnp.exp(dmat - m_t)
        wi = jnp.exp(inter - m_t)
        wqk = w * _mm_nt(q, k)
        c_st = c1_ref[0, h]
        n_st = n1_ref[0, h:h + 1, :]
        num = _mm(wqk, v) + wi * _mm(q, c_st)
        den = jnp.sum(wqk, axis=1, keepdims=True) + wi * jnp.sum(q * n_st, axis=1, keepdims=True)
        hm = num / jnp.maximum(jnp.abs(den), 1.0)
        m_new = m_t[c - 1:c, :]
        b_last = b_col[c - 1:c, :]
        ws = jnp.exp(b_last - b_col + ig_col - m_new)
        decay = jnp.exp(b_last + m_prev - m_new)
        kw = k * ws
        c1_ref[0, h] = decay * c_st + _mm_tn(kw, v)
        n1_ref[0, h:h + 1, :] = decay * n_st + jnp.sum(kw, axis=0, keepdims=True)
        m1_ref[0, :, h:h + 1] = m_new
        ms = jnp.mean(hm * hm, axis=1, keepdims=True)
        y_ref[0, :, lo:hi] = hm * lax.rsqrt(ms + EPS) * wn_ref[:, lo:hi] * _sigmoid(og)

    xbuf[SUBLANES:SUBLANES + c, :] = z_ref[0, :, OFF_GQKV:OFF_GQKV + 3 * G_W]
    first = SUBLANES - (CONV_W - 1)
    pre = cw_ref[0:1, :] * xbuf[first:first + c, :]
    for j in range(1, CONV_W):
        pre = pre + cw_ref[j:j + 1, :] * xbuf[first + j:first + j + c, :]
    conv = pre * _sigmoid(pre)
    carried = xbuf[c:c + SUBLANES, :]
    xbuf[0:SUBLANES, :] = carried
    cv1_ref[0] = carried

    for h in range(G_HEADS):
        lo, hi = h * HEAD_DIM, (h + 1) * HEAD_DIM
        gq = conv[:, lo:hi]
        gk = conv[:, G_W + lo:G_W + hi]
        gv = conv[:, 2 * G_W + lo:2 * G_W + hi]
        gq = gq * lax.rsqrt(jnp.sum(gq * gq, axis=1, keepdims=True) + EPS) * QK_SCALE
        gk = gk * lax.rsqrt(jnp.sum(gk * gk, axis=1, keepdims=True) + EPS)
        beta = gates[:, GATE_GB + h:GATE_GB + h + 1]
        g_col = cs[:, GATE_GA + h:GATE_GA + h + 1]
        g_row = cs_t[GATE_GA + h:GATE_GA + h + 1, :]
        ediff = jnp.exp(jnp.where(incl, g_col - g_row, 0.0))
        kb = gk * beta
        lm = jnp.where(strict, _mm_nt(kb, gk) * ediff, 0.0)
        tinv = _unit_lower_inverse(lm, eye_c, row, col, c)
        eg = jnp.exp(g_col)
        u = _mm(tinv, gv * beta)
        wk = _mm(tinv, kb * eg)
        s_st = s1_ref[0, h]
        v_new = u - _mm(wk, s_st)
        attn = jnp.where(incl, _mm_nt(gq, gk) * ediff, 0.0)
        o = _mm(gq * eg, s_st) + _mm(attn, v_new)
        gl = g_col[c - 1:c, :]
        s1_ref[0, h] = s_st * jnp.exp(gl) + _mm_tn(gk * jnp.exp(gl - g_col), v_new)
        ms = jnp.mean(o * o, axis=1, keepdims=True)
        gg = z_ref[0, :, OFF_GG + lo:OFF_GG + hi]
        y_ref[0, :, M_W + lo:M_W + hi] = (o * lax.rsqrt(ms + EPS) * wn_ref[:, M_W + lo:M_W + hi]
                                          * (gg * _sigmoid(gg)))

    hf = z_ref[0, :, OFF_HF:OFF_HF + H_W]
    ls = -(jnp.maximum(-hf, 0.0) + _log1pexp_negabs(hf))
    if layer == 0:
        lf = ls
        kk = _sigmoid(-hf)
    else:
        lg = lbl_ref[...]
        ex = jnp.exp(lg - jnp.max(lg, axis=0, keepdims=True))
        sm = ex / jnp.sum(ex, axis=0, keepdims=True)
        cum = sm[0:1, :]
        for j in range(1, layer + 1):
            cum = cum + sm[j:j + 1, :]
        lb = cum - sm[0:1, :]
        a = jnp.log(lb)
        b = jnp.log1p(-lb) + ls
        lf = jnp.maximum(a, b) + _log1pexp_negabs(a - b)
        kk = (1.0 - lb) * _sigmoid(-hf)
    gcum = _mm_f32(tri, lf)
    ones_d = jnp.ones((HEAD_DIM, HEAD_DIM), F32)
    trow = lax.broadcasted_iota(jnp.int32, (sb, HEAD_DIM), 0)
    for h in range(H_HEADS):
        lo, hi = h * HEAD_DIM, (h + 1) * HEAD_DIM
        q = z_ref[0, :, OFF_HQ + lo:OFF_HQ + hi]
        k = kk[:, lo:hi]
        v = z_ref[0, :, OFF_HI + lo:OFF_HI + hi]
        gh = gcum[:, lo:hi]
        s_st = h1_ref[0, h]
        inter = _mm(q * jnp.exp(gh), s_st)
        blocks = []
        for blk in range(c // sb):
            r0 = blk * sb
            qi = q[r0:r0 + sb]
            ki = k[r0:r0 + sb]
            vi = v[r0:r0 + sb]
            gi = gh[r0:r0 + sb]
            prods = []
            for s in range(sb):
                e = jnp.exp(jnp.where(trow >= s, gi - gi[s:s + 1], -jnp.inf))
                prods.append(qi * ki[s:s + 1] * e)
            rs = _mm(jnp.concatenate(prods, axis=0), ones_d)
            oi = rs[0:sb] * vi[0:1]
            for s in range(1, sb):
                oi = oi + rs[s * sb:(s + 1) * sb] * vi[s:s + 1]
            if blk > 0:
                ref = gh[r0:r0 + 1]
                qt = qi * jnp.exp(gi - ref)
                kt = k[0:r0] * jnp.exp(ref - gh[0:r0])
                oi = oi + _mm(_mm_nt(qt, kt), v[0:r0])
            blocks.append(oi)
        o = inter + (blocks[0] if len(blocks) == 1 else jnp.concatenate(blocks, axis=0))
        gl = gh[c - 1:c]
        h1_ref[0, h] = _row_to_col(jnp.exp(gl), eye_d) * s_st + _mm_tn(k * jnp.exp(gl - gh), v)
        ms = jnp.mean(o * o, axis=1, keepdims=True)
        hg = z_ref[0, :, OFF_HG + lo:OFF_HG + hi]
        y_ref[0, :, M_W + G_W + lo:M_W + G_W + hi] = (o * lax.rsqrt(ms + EPS)
                                                      * wn_ref[:, M_W + G_W + lo:M_W + G_W + hi]
                                                      * (hg * _sigmoid(hg)))


def _mixer_call(layer, z, prm, cw, lbl, wn, st_c, st_n, st_m, st_s, st_cv, st_h, st_layer, c, sb):
    nb, seq, _ = z.shape
    nchunks = seq // c
    kern = functools.partial(_mixer_kernel, layer, c, sb)
    const2 = lambda b, n: (0, 0)
    st5 = lambda b, n: (st_layer, b, 0, 0, 0)
    st4 = lambda b, n: (st_layer, b, 0, 0)
    o4 = lambda b, n: (b, 0, 0, 0)
    o3 = lambda b, n: (b, 0, 0)
    out_shapes = (
        jax.ShapeDtypeStruct((nb, seq, D_MIX), F32),
        jax.ShapeDtypeStruct((nb, M_HEADS, HEAD_DIM, HEAD_DIM), F32),
        jax.ShapeDtypeStruct((nb, M_HEADS, HEAD_DIM), F32),
        jax.ShapeDtypeStruct((nb, 1, M_HEADS), F32),
        jax.ShapeDtypeStruct((nb, G_HEADS, HEAD_DIM, HEAD_DIM), F32),
        jax.ShapeDtypeStruct((nb, SUBLANES, 3 * G_W), F32),
        jax.ShapeDtypeStruct((nb, H_HEADS, HEAD_DIM, HEAD_DIM), F32),
    )
    return pl.pallas_call(
        kern,
        grid=(nb, nchunks),
        in_specs=[
            pl.BlockSpec((1, c, Z_W), lambda b, n: (b, n, 0)),
            pl.BlockSpec(prm.shape, const2),
            pl.BlockSpec(cw.shape, const2),
            pl.BlockSpec(lbl.shape, const2),
            pl.BlockSpec(wn.shape, const2),
            pl.BlockSpec((None, 1, M_HEADS, HEAD_DIM, HEAD_DIM), st5),
            pl.BlockSpec((None, 1, M_HEADS, HEAD_DIM), st4),
            pl.BlockSpec((None, 1, 1, M_HEADS), st4),
            pl.BlockSpec((None, 1, G_HEADS, HEAD_DIM, HEAD_DIM), st5),
            pl.BlockSpec((None, 1, SUBLANES, 3 * G_W), st4),
            pl.BlockSpec((None, 1, H_HEADS, HEAD_DIM, HEAD_DIM), st5),
        ],
        out_specs=(
            pl.BlockSpec((1, c, D_MIX), lambda b, n: (b, n, 0)),
            pl.BlockSpec((1, M_HEADS, HEAD_DIM, HEAD_DIM), o4),
            pl.BlockSpec((1, M_HEADS, HEAD_DIM), o3),
            pl.BlockSpec((1, 1, M_HEADS), o3),
            pl.BlockSpec((1, G_HEADS, HEAD_DIM, HEAD_DIM), o4),
            pl.BlockSpec((1, SUBLANES, 3 * G_W), o3),
            pl.BlockSpec((1, H_HEADS, HEAD_DIM, HEAD_DIM), o4),
        ),
        out_shape=out_shapes,
        scratch_shapes=[pltpu.VMEM((SUBLANES + c, 3 * G_W), F32)],
        compiler_params=pltpu.CompilerParams(
            dimension_semantics=("parallel", "arbitrary"), vmem_limit_bytes=VMEM_LIMIT_BYTES),
        name="mixer",
    )(z, prm, cw, lbl, wn, st_c, st_n, st_m, st_s, st_cv, st_h)


def _permute_in_proj(w_in, b_in):
    o = 0
    pieces = {}
    for name, width in (("mq", M_W), ("mk", M_W), ("mv", M_W), ("mo", M_W), ("mi", M_HEADS), ("mf", M_HEADS),
                        ("gqkv", 3 * G_W), ("gg", G_W), ("gb", G_HEADS), ("ga", G_HEADS),
                        ("hq", H_W), ("hf", H_W), ("hi", H_W), ("hg", H_W)):
        pieces[name] = (o, o + width)
        o += width
    order = ("mq", "mk", "mv", "mo", "gqkv", "gg", "hq", "hf", "hi", "hg", "mi", "mf", "gb", "ga")

    def perm(a):
        cols = [a[..., pieces[n][0]:pieces[n][1]] for n in order]
        pad = jnp.zeros(a.shape[:-1] + (LANES - GATE_END,), a.dtype)
        return jnp.concatenate(cols + [pad], axis=-1)

    return perm(w_in).astype(BF16), perm(b_in).reshape(DEPTH, 1, Z_W)


def _gate_params(mlstm_f_bias, gdn_A_log, gdn_dt_bias):
    prm = jnp.zeros((DEPTH, SUBLANES, LANES), F32)
    prm = prm.at[:, 0, GATE_MF:GATE_MF + M_HEADS].set(mlstm_f_bias.astype(F32))
    prm = prm.at[:, 0, GATE_GA:GATE_GA + G_HEADS].set(gdn_dt_bias.astype(F32))
    prm = prm.at[:, 1, GATE_GA:GATE_GA + G_HEADS].set(gdn_A_log.astype(F32))
    return prm


def _trunk(x, mod, states, st_is_zero, params, c, sb, in_bt, ffn_bt):
    (npre_mix, npost_mix, w_in, b_in, prm, conv_w, lb_logits, out_norm, w_out, npre_ffn, npost_ffn, w_ff1,
     w_ff2) = params
    nb, seq, _ = x.shape
    st_c, st_n, st_m, st_s, st_cv, st_h = states
    new = ([], [], [], [], [], [])
    for l in range(DEPTH):
        z = _inproj_call(x, mod[l], npre_mix[l], w_in[l], b_in[l], *in_bt)
        outs = _mixer_call(l, z.reshape(nb, seq, Z_W), prm[l], conv_w[l], lb_logits, out_norm[l],
                           st_c, st_n, st_m, st_s, st_cv, st_h, 0 if st_is_zero else l, c, sb)
        ymix = outs[0].reshape(nb * seq, D_MIX)
        x = _ffn_call(x, ymix, mod[l], w_out[l], w_ff1[l], w_ff2[l], npost_mix[l], npre_ffn[l], npost_ffn[l],
                      *ffn_bt)
        for lst, s in zip(new, outs[1:]):
            lst.append(s)
    c1, n1, m1, s1, cv1, h1 = (jnp.stack(s) for s in new)
    return x, (c1, n1, m1.reshape(DEPTH, nb, M_HEADS), s1, cv1[:, :, SUBLANES - (CONV_W - 1):, :], h1)


def kernel(x_prompt, x_sample, state_mlstm_C, state_mlstm_n, state_mlstm_m, state_gdn_S, state_gdn_conv,
           state_hgrn_S, c_prompt, c_sample, w_ada, b_ada, norm_pre_mix, norm_post_mix, w_in, b_in,
           mlstm_f_bias, gdn_conv_w, gdn_A_log, gdn_dt_bias, hgrn_lb_logits, mix_out_norm, w_out,
           norm_pre_ffn, norm_post_ffn, w_ff1, w_ff2):
    bp, seq_p, _ = x_prompt.shape
    bs, seq_s, _ = x_sample.shape

    mod = _ada_call(jnp.concatenate([c_prompt, c_sample], axis=0), w_ada, b_ada)
    mod_p = mod[:, :bp].reshape(DEPTH, bp, 1, 6 * D_MODEL)
    mod_s = mod[:, bp:].reshape(DEPTH, bs, 1, 6 * D_MODEL)

    w_in_p, b_in_p = _permute_in_proj(w_in, b_in)
    row = lambda a: a.reshape(DEPTH, 1, a.shape[-1]).astype(F32)
    params = (row(norm_pre_mix), row(norm_post_mix), w_in_p, b_in_p,
              _gate_params(mlstm_f_bias, gdn_A_log, gdn_dt_bias), gdn_conv_w.astype(F32),
              hgrn_lb_logits.astype(F32), row(mix_out_norm), w_out.astype(BF16), row(norm_pre_ffn),
              row(norm_post_ffn), w_ff1.astype(BF16), w_ff2.astype(BF16))

    pad_conv = ((0, 0), (0, 0), (SUBLANES - (CONV_W - 1), 0), (0, 0))
    zeros = (jnp.zeros((1, bp, M_HEADS, HEAD_DIM, HEAD_DIM), F32),
             jnp.zeros((1, bp, M_HEADS, HEAD_DIM), F32),
             jnp.zeros((1, bp, 1, M_HEADS), F32),
             jnp.zeros((1, bp, G_HEADS, HEAD_DIM, HEAD_DIM), F32),
             jnp.zeros((1, bp, SUBLANES, 3 * G_W), F32),
             jnp.zeros((1, bp, H_HEADS, HEAD_DIM, HEAD_DIM), F32))
    past = (state_mlstm_C.astype(F32), state_mlstm_n.astype(F32),
            state_mlstm_m.astype(F32).reshape(DEPTH, bs, 1, M_HEADS), state_gdn_S.astype(F32),
            jnp.pad(state_gdn_conv.astype(F32), pad_conv), state_hgrn_S.astype(F32))

    y_p, st_p = _trunk(x_prompt, mod_p, zeros, True, params, PROMPT_CHUNK, HGRN_SUBBLOCK,
                       (1, IN_TILE), (1, FFN_TILE))
    y_s, st_s = _trunk(x_sample, mod_s, past, False, params, seq_s, min(seq_s, HGRN_SUBBLOCK),
                       (IN_TILE // seq_s, seq_s), (FFN_TILE // seq_s, seq_s))
    return (y_p, y_s) + st_p + st_s
```

```python
import functools

import jax
import jax.numpy as jnp
from jax import lax
from jax.experimental import pallas as pl
from jax.experimental.pallas import tpu as pltpu

F32 = jnp.float32
BF16 = jnp.bfloat16

D_MODEL = 1024
DEPTH = 4
HEAD_DIM = 64
M_HEADS = 6
G_HEADS = 6
H_HEADS = 4
M_W = M_HEADS * HEAD_DIM
G_W = G_HEADS * HEAD_DIM
H_W = H_HEADS * HEAD_DIM
D_MIX = M_W + G_W + H_W
CONV_W = 4
D_FF = 4 * D_MODEL
EPS = 1e-6
QK_SCALE = HEAD_DIM ** -0.5

LANES = 128
SUBLANES = 8
VMEM_LIMIT_BYTES = 56 * 1024 * 1024

OFF_MQ = 0
OFF_MK = OFF_MQ + M_W
OFF_MV = OFF_MK + M_W
OFF_MO = OFF_MV + M_W
OFF_GQKV = OFF_MO + M_W
OFF_GG = OFF_GQKV + 3 * G_W
OFF_HQ = OFF_GG + G_W
OFF_HF = OFF_HQ + H_W
OFF_HI = OFF_HF + H_W
OFF_HG = OFF_HI + H_W
OFF_GATE = OFF_HG + H_W
Z_W = OFF_GATE + LANES
GATE_MI = 0
GATE_MF = GATE_MI + M_HEADS
GATE_GB = GATE_MF + M_HEADS
GATE_GA = GATE_GB + G_HEADS
GATE_END = GATE_GA + G_HEADS

PROMPT_CHUNK = 64
HGRN_SUBBLOCK = 16
PROMPT_GROUP = 4
SAMPLE_GROUP = 16
IN_TILE = 512
FFN_TILE = 512
FF_BLOCK = 1024
ADA_BLOCK = 1536


def _mm(a, b):
    return jnp.dot(a.astype(BF16), b.astype(BF16), preferred_element_type=F32)


def _mm_f32(a, b):
    return jnp.dot(a, b, preferred_element_type=F32, precision=lax.Precision.HIGHEST)


def _bmm(a, b):
    return jnp.einsum("nmk,nkp->nmp", a.astype(BF16), b.astype(BF16), preferred_element_type=F32)


def _bmm_nt(a, b):
    return jnp.einsum("nmk,npk->nmp", a.astype(BF16), b.astype(BF16), preferred_element_type=F32)


def _bmm_tn(a, b):
    return jnp.einsum("nsk,nsp->nkp", a.astype(BF16), b.astype(BF16), preferred_element_type=F32)


def _sigmoid(x):
    return 1.0 / (1.0 + jnp.exp(-x))


def _log1pexp_negabs(x):
    return jnp.log1p(jnp.exp(-jnp.abs(x)))


def _rms(x, w):
    ms = jnp.mean(x * x, axis=-1, keepdims=True)
    return x * lax.rsqrt(ms + EPS) * w


def _ada_kernel(c_ref, w_ref, b_ref, o_ref):
    o_ref[...] = jnp.dot(c_ref[...].astype(BF16), w_ref[...].astype(BF16), preferred_element_type=F32) + b_ref[...]


def _ada_call(c_all, w_ada, b_ada):
    rows = c_all.shape[0]
    nblk = (6 * D_MODEL) // ADA_BLOCK
    return pl.pallas_call(
        _ada_kernel,
        grid=(DEPTH, nblk),
        in_specs=[
            pl.BlockSpec((rows, D_MODEL), lambda l, j: (0, 0)),
            pl.BlockSpec((None, D_MODEL, ADA_BLOCK), lambda l, j: (l, 0, j)),
            pl.BlockSpec((None, 1, ADA_BLOCK), lambda l, j: (l, 0, j)),
        ],
        out_specs=pl.BlockSpec((None, rows, ADA_BLOCK), lambda l, j: (l, 0, j)),
        out_shape=jax.ShapeDtypeStruct((DEPTH, rows, 6 * D_MODEL), F32),
        compiler_params=pltpu.CompilerParams(
            dimension_semantics=("parallel", "parallel"), vmem_limit_bytes=VMEM_LIMIT_BYTES),
        name="adaln",
    )(c_all, w_ada, b_ada.reshape(DEPTH, 1, 6 * D_MODEL))


def _inproj_kernel(x_ref, mod_ref, npre_ref, w_ref, b_ref, z_ref):
    bb, t, _ = x_ref.shape
    mod = mod_ref[...]
    sh1 = mod[:, :, 0:D_MODEL]
    sc1 = mod[:, :, D_MODEL:2 * D_MODEL]
    h = _rms(x_ref[...], npre_ref[...]) * (1.0 + sc1) + sh1
    hb = h.reshape(bb * t, D_MODEL).astype(BF16)
    z_ref[...] = jnp.dot(hb, w_ref[...], preferred_element_type=F32) + b_ref[...]


def _inproj_call(x, mod, npre, w, b, bb, t):
    nb, seq, _ = x.shape
    grid = (nb // bb, seq // t)
    nseq = seq // t
    return pl.pallas_call(
        _inproj_kernel,
        grid=grid,
        in_specs=[
            pl.BlockSpec((bb, t, D_MODEL), lambda i, j: (i, j, 0)),
            pl.BlockSpec((bb, 1, 6 * D_MODEL), lambda i, j: (i, 0, 0)),
            pl.BlockSpec((1, D_MODEL), lambda i, j: (0, 0)),
            pl.BlockSpec((D_MODEL, Z_W), lambda i, j: (0, 0)),
            pl.BlockSpec((1, Z_W), lambda i, j: (0, 0)),
        ],
        out_specs=pl.BlockSpec((bb * t, Z_W), lambda i, j: (i * nseq + j, 0)),
        out_shape=jax.ShapeDtypeStruct((nb * seq, Z_W), F32),
        compiler_params=pltpu.CompilerParams(
            dimension_semantics=("parallel", "parallel"), vmem_limit_bytes=VMEM_LIMIT_BYTES),
        name="inproj",
    )(x, mod, npre, w, b)


def _ffn_kernel(x_ref, ym_ref, mod_ref, wout_ref, w1_ref, w2_ref, npost_ref, npre_ref, npostf_ref,
                o_ref, x1_s, h2_s, acc_s):
    j = pl.program_id(2)
    bb, t, _ = x_ref.shape

    @pl.when(j == 0)
    def _():
        mod = mod_ref[...]
        g1 = mod[:, :, 2 * D_MODEL:3 * D_MODEL]
        sh2 = mod[:, :, 3 * D_MODEL:4 * D_MODEL]
        sc2 = mod[:, :, 4 * D_MODEL:5 * D_MODEL]
        y = jnp.dot(ym_ref[...].astype(BF16), wout_ref[...], preferred_element_type=F32)
        x1 = x_ref[...] + g1 * _rms(y.reshape(bb, t, D_MODEL), npost_ref[...])
        x1_s[...] = x1
        h2 = _rms(x1, npre_ref[...]) * (1.0 + sc2) + sh2
        h2_s[...] = h2.reshape(bb * t, D_MODEL).astype(BF16)
        acc_s[...] = jnp.zeros_like(acc_s)

    a = jnp.dot(h2_s[...], w1_ref[...], preferred_element_type=F32)
    a = jnp.square(jnp.maximum(a, 0.0))
    acc_s[...] += jnp.dot(a.astype(BF16), w2_ref[...], preferred_element_type=F32)

    @pl.when(j == pl.num_programs(2) - 1)
    def _():
        g2 = mod_ref[...][:, :, 5 * D_MODEL:6 * D_MODEL]
        f = acc_s[...].reshape(bb, t, D_MODEL)
        o_ref[...] = x1_s[...] + g2 * _rms(f, npostf_ref[...])


def _ffn_call(x, ymix, mod, wout, w1, w2, npost, npre, npostf, bb, t):
    nb, seq, _ = x.shape
    nseq = seq // t
    nff = D_FF // FF_BLOCK
    return pl.pallas_call(
        _ffn_kernel,
        grid=(nb // bb, nseq, nff),
        in_specs=[
            pl.BlockSpec((bb, t, D_MODEL), lambda i, s, j: (i, s, 0)),
            pl.BlockSpec((bb * t, D_MIX), lambda i, s, j: (i * nseq + s, 0)),
            pl.BlockSpec((bb, 1, 6 * D_MODEL), lambda i, s, j: (i, 0, 0)),
            pl.BlockSpec((D_MIX, D_MODEL), lambda i, s, j: (0, 0)),
            pl.BlockSpec((D_MODEL, FF_BLOCK), lambda i, s, j: (0, j)),
            pl.BlockSpec((FF_BLOCK, D_MODEL), lambda i, s, j: (j, 0)),
            pl.BlockSpec((1, D_MODEL), lambda i, s, j: (0, 0)),
            pl.BlockSpec((1, D_MODEL), lambda i, s, j: (0, 0)),
            pl.BlockSpec((1, D_MODEL), lambda i, s, j: (0, 0)),
        ],
        out_specs=pl.BlockSpec((bb, t, D_MODEL), lambda i, s, j: (i, s, 0)),
        out_shape=jax.ShapeDtypeStruct(x.shape, F32),
        scratch_shapes=[
            pltpu.VMEM((bb, t, D_MODEL), F32),
            pltpu.VMEM((bb * t, D_MODEL), BF16),
            pltpu.VMEM((bb * t, D_MODEL), F32),
        ],
        compiler_params=pltpu.CompilerParams(
            dimension_semantics=("parallel", "parallel", "arbitrary"), vmem_limit_bytes=VMEM_LIMIT_BYTES),
        name="outffn",
    )(x, ymix, mod, wout, w1, w2, npost, npre, npostf)


def _unit_lower_inverse(lm, eye, row, col, c):
    t = None
    s = 1
    while s < c:
        same = (row // (2 * s)) == (col // (2 * s))
        off = same & ((row % (2 * s)) >= s) & ((col % (2 * s)) < s)
        coff = jnp.where(off, lm, 0.0)
        t = eye - coff if t is None else t - _bmm(_bmm(t, coff), t)
        s *= 2
    return t


def _head_rmsnorm_gate(o, wn, gate):
    ms = jnp.mean(o * o, axis=2, keepdims=True)
    return o * lax.rsqrt(ms + EPS) * wn * gate


def _mixer_kernel(layer, c, sb, grp,
                  z_ref, prm_ref, cw_ref, lbl_ref, wn_ref, c0_ref, n0_ref, m0_ref, s0_ref, cv0_ref, h0_ref,
                  y_ref, c1_ref, n1_ref, m1_ref, s1_ref, cv1_ref, h1_ref, xbuf):
    step = pl.program_id(1)

    @pl.when(step == 0)
    def _():
        c1_ref[...] = c0_ref[...]
        n1_ref[...] = n0_ref[...]
        m1_ref[...] = m0_ref[...]
        s1_ref[...] = s0_ref[...]
        h1_ref[...] = h0_ref[...]
        xbuf[:, 0:SUBLANES, :] = cv0_ref[...]

    row = lax.broadcasted_iota(jnp.int32, (c, c), 0)
    col = lax.broadcasted_iota(jnp.int32, (c, c), 1)
    incl = row >= col
    strict = row > col
    eye_c = (row == col).astype(F32)
    tri = incl.astype(F32)
    r64 = lax.broadcasted_iota(jnp.int32, (HEAD_DIM, HEAD_DIM), 0)
    c64 = lax.broadcasted_iota(jnp.int32, (HEAD_DIM, HEAD_DIM), 1)
    eye_d = (r64 == c64).astype(F32)

    def heads_of(ref_or_val, off, nh):
        return jnp.concatenate(
            [ref_or_val[:, :, off + h * HEAD_DIM:off + (h + 1) * HEAD_DIM] for h in range(nh)], axis=0)

    def cols_of(a, off, nh):
        return jnp.concatenate([a[:, :, off + h:off + h + 1] for h in range(nh)], axis=0)

    def rows_of(a_t, off, nh):
        return jnp.concatenate([a_t[:, off + h:off + h + 1, :] for h in range(nh)], axis=0)

    def norm_weights(off, nh):
        return jnp.concatenate(
            [jnp.broadcast_to(wn_ref[:, off + h * HEAD_DIM:off + (h + 1) * HEAD_DIM][None], (grp, 1, HEAD_DIM))
             for h in range(nh)], axis=0)

    def store_heads(val, off, nh):
        for h in range(nh):
            y_ref[:, :, off + h * HEAD_DIM:off + (h + 1) * HEAD_DIM] = val[h * grp:(h + 1) * grp]

    zg = z_ref[:, :, OFF_GATE:OFF_GATE + LANES]
    lane = lax.broadcasted_iota(jnp.int32, (1, 1, LANES), 2)
    tg = zg + prm_ref[0:1, :]
    tail = _log1pexp_negabs(tg)
    logsig = -(jnp.maximum(-tg, 0.0) + tail)
    splus = jnp.maximum(tg, 0.0) + tail
    lgv = -jnp.exp(prm_ref[1:2, :]) * splus
    gates = jnp.where(lane < GATE_MF, zg,
                      jnp.where(lane < GATE_GB, logsig,
                                jnp.where(lane < GATE_GA, _sigmoid(tg),
                                          jnp.where(lane < GATE_END, lgv, 0.0))))
    cs = jnp.stack([_mm_f32(tri, gates[g]) for g in range(grp)])
    gates_t = jnp.stack([gates[g].T for g in range(grp)])
    cs_t = jnp.stack([cs[g].T for g in range(grp)])

    q = heads_of(z_ref, OFF_MQ, M_HEADS)
    k = heads_of(z_ref, OFF_MK, M_HEADS) * QK_SCALE
    v = heads_of(z_ref, OFF_MV, M_HEADS)
    ig_row = rows_of(gates_t, GATE_MI, M_HEADS)
    ig_col = cols_of(gates, GATE_MI, M_HEADS)
    b_col = cols_of(cs, GATE_MF, M_HEADS)
    b_row = rows_of(cs_t, GATE_MF, M_HEADS)
    m_prev = jnp.concatenate([m1_ref[:, :, h:h + 1] for h in range(M_HEADS)], axis=0)
    dmat = jnp.where(incl, b_col - b_row + ig_row, -jnp.inf)
    inter = b_col + m_prev
    m_t = jnp.maximum(inter, jnp.max(dmat, axis=2, keepdims=True))
    w = jnp.exp(dmat - m_t)
    wi = jnp.exp(inter - m_t)
    wqk = w * _bmm_nt(q, k)
    c_st = jnp.concatenate([c1_ref[:, h] for h in range(M_HEADS)], axis=0)
    n_st = jnp.concatenate([n1_ref[:, h:h + 1, :] for h in range(M_HEADS)], axis=0)
    num = _bmm(wqk, v) + wi * _bmm(q, c_st)
    den = jnp.sum(wqk, axis=2, keepdims=True) + wi * jnp.sum(q * n_st, axis=2, keepdims=True)
    hm = num / jnp.maximum(jnp.abs(den), 1.0)
    m_new = m_t[:, c - 1:c, :]
    b_last = b_col[:, c - 1:c, :]
    ws = jnp.exp(b_last - b_col + ig_col - m_new)
    decay = jnp.exp(b_last + m_prev - m_new)
    kw = k * ws
    c_new = decay * c_st + _bmm_tn(kw, v)
    n_new = decay * n_st + jnp.sum(kw, axis=1, keepdims=True)
    for h in range(M_HEADS):
        c1_ref[:, h] = c_new[h * grp:(h + 1) * grp]
        n1_ref[:, h:h + 1, :] = n_new[h * grp:(h + 1) * grp]
        m1_ref[:, :, h:h + 1] = m_new[h * grp:(h + 1) * grp]
    store_heads(_head_rmsnorm_gate(hm, norm_weights(0, M_HEADS), _sigmoid(heads_of(z_ref, OFF_MO, M_HEADS))),
                0, M_HEADS)

    xbuf[:, SUBLANES:SUBLANES + c, :] = z_ref[:, :, OFF_GQKV:OFF_GQKV + 3 * G_W]
    first = SUBLANES - (CONV_W - 1)
    pre = cw_ref[0:1, :] * xbuf[:, first:first + c, :]
    for j in range(1, CONV_W):
        pre = pre + cw_ref[j:j + 1, :] * xbuf[:, first + j:first + j + c, :]
    conv = pre * _sigmoid(pre)
    carried = xbuf[:, c:c + SUBLANES, :]
    xbuf[:, 0:SUBLANES, :] = carried
    cv1_ref[...] = carried

    gq = heads_of(conv, 0, G_HEADS)
    gk = heads_of(conv, G_W, G_HEADS)
    gv = heads_of(conv, 2 * G_W, G_HEADS)
    gq = gq * lax.rsqrt(jnp.sum(gq * gq, axis=2, keepdims=True) + EPS) * QK_SCALE
    gk = gk * lax.rsqrt(jnp.sum(gk * gk, axis=2, keepdims=True) + EPS)
    beta = cols_of(gates, GATE_GB, G_HEADS)
    g_col = cols_of(cs, GATE_GA, G_HEADS)
    g_row = rows_of(cs_t, GATE_GA, G_HEADS)
    ediff = jnp.exp(jnp.where(incl, g_col - g_row, 0.0))
    kb = gk * beta
    lm = jnp.where(strict, _bmm_nt(kb, gk) * ediff, 0.0)
    tinv = _unit_lower_inverse(lm, eye_c, row, col, c)
    eg = jnp.exp(g_col)
    u = _bmm(tinv, gv * beta)
    wk = _bmm(tinv, kb * eg)
    s_st = jnp.concatenate([s1_ref[:, h] for h in range(G_HEADS)], axis=0)
    v_new = u - _bmm(wk, s_st)
    attn = jnp.where(incl, _bmm_nt(gq, gk) * ediff, 0.0)
    o = _bmm(gq * eg, s_st) + _bmm(attn, v_new)
    gl = g_col[:, c - 1:c, :]
    s_new = s_st * jnp.exp(gl) + _bmm_tn(gk * jnp.exp(gl - g_col), v_new)
    for h in range(G_HEADS):
        s1_ref[:, h] = s_new[h * grp:(h + 1) * grp]
    gg = heads_of(z_ref, OFF_GG, G_HEADS)
    store_heads(_head_rmsnorm_gate(o, norm_weights(M_W, G_HEADS), gg * _sigmoid(gg)), M_W, G_HEADS)

    hf = z_ref[:, :, OFF_HF:OFF_HF + H_W]
    ls = -(jnp.maximum(-hf, 0.0) + _log1pexp_negabs(hf))
    if layer == 0:
        lf = ls
        kk = _sigmoid(-hf)
    else:
        lg = lbl_ref[...]
        ex = jnp.exp(lg - jnp.max(lg, axis=0, keepdims=True))
        sm = ex / jnp.sum(ex, axis=0, keepdims=True)
        cum = sm[0:1, :]
        for j in range(1, layer + 1):
            cum = cum + sm[j:j + 1, :]
        lb = cum - sm[0:1, :]
        a = jnp.log(lb)
        b = jnp.log1p(-lb) + ls
        lf = jnp.maximum(a, b) + _log1pexp_negabs(a - b)
        kk = (1.0 - lb) * _sigmoid(-hf)
    gcum = jnp.stack([_mm_f32(tri, lf[g]) for g in range(grp)])
    nh = H_HEADS * grp
    ones_d = jnp.ones((HEAD_DIM, HEAD_DIM), F32)
    trow = lax.broadcasted_iota(jnp.int32, (sb, HEAD_DIM), 0)
    q = heads_of(z_ref, OFF_HQ, H_HEADS)
    k = heads_of(kk, 0, H_HEADS)
    v = heads_of(z_ref, OFF_HI, H_HEADS)
    gh = heads_of(gcum, 0, H_HEADS)
    s_st = jnp.concatenate([h1_ref[:, h] for h in range(H_HEADS)], axis=0)
    inter = _bmm(q * jnp.exp(gh), s_st)
    blocks = []
    for blk in range(c // sb):
        r0 = blk * sb
        qi = q[:, r0:r0 + sb]
        ki = k[:, r0:r0 + sb]
        vi = v[:, r0:r0 + sb]
        gi = gh[:, r0:r0 + sb]
        prods = []
        for s in range(sb):
            e = jnp.exp(jnp.where(trow >= s, gi - gi[:, s:s + 1], -jnp.inf))
            prods.append(qi * ki[:, s:s + 1] * e)
        stacked = jnp.concatenate(prods, axis=1).reshape(nh * sb * sb, HEAD_DIM)
        rs = _mm(stacked, ones_d).reshape(nh, sb * sb, HEAD_DIM)
        oi = rs[:, 0:sb] * vi[:, 0:1]
        for s in range(1, sb):
            oi = oi + rs[:, s * sb:(s + 1) * sb] * vi[:, s:s + 1]
        if blk > 0:
            ref = gh[:, r0:r0 + 1]
            qt = qi * jnp.exp(gi - ref)
            kt = k[:, 0:r0] * jnp.exp(ref - gh[:, 0:r0])
            oi = oi + _bmm(_bmm_nt(qt, kt), v[:, 0:r0])
        blocks.append(oi)
    o = inter + (blocks[0] if len(blocks) == 1 else jnp.concatenate(blocks, axis=1))
    gl = gh[:, c - 1:c]
    decay_col = jnp.sum(eye_d * jnp.exp(gl), axis=2, keepdims=True)
    s_new = decay_col * s_st + _bmm_tn(k * jnp.exp(gl - gh), v)
    for h in range(H_HEADS):
        h1_ref[:, h] = s_new[h * grp:(h + 1) * grp]
    hg = heads_of(z_ref, OFF_HG, H_HEADS)
    store_heads(_head_rmsnorm_gate(o, norm_weights(M_W + G_W, H_HEADS), hg * _sigmoid(hg)), M_W + G_W, H_HEADS)


def _mixer_call(layer, z, prm, cw, lbl, wn, st_c, st_n, st_m, st_s, st_cv, st_h, st_layer, c, sb, grp):
    nb, seq, _ = z.shape
    nchunks = seq // c
    kern = functools.partial(_mixer_kernel, layer, c, sb, grp)
    const2 = lambda b, n: (0, 0)
    st5 = lambda b, n: (st_layer, b, 0, 0, 0)
    st4 = lambda b, n: (st_layer, b, 0, 0)
    o4 = lambda b, n: (b, 0, 0, 0)
    o3 = lambda b, n: (b, 0, 0)
    out_shapes = (
        jax.ShapeDtypeStruct((nb, seq, D_MIX), F32),
        jax.ShapeDtypeStruct((nb, M_HEADS, HEAD_DIM, HEAD_DIM), F32),
        jax.ShapeDtypeStruct((nb, M_HEADS, HEAD_DIM), F32),
        jax.ShapeDtypeStruct((nb, 1, M_HEADS), F32),
        jax.ShapeDtypeStruct((nb, G_HEADS, HEAD_DIM, HEAD_DIM), F32),
        jax.ShapeDtypeStruct((nb, SUBLANES, 3 * G_W), F32),
        jax.ShapeDtypeStruct((nb, H_HEADS, HEAD_DIM, HEAD_DIM), F32),
    )
    return pl.pallas_call(
        kern,
        grid=(nb // grp, nchunks),
        in_specs=[
            pl.BlockSpec((grp, c, Z_W), lambda b, n: (b, n, 0)),
            pl.BlockSpec(prm.shape, const2),
            pl.BlockSpec(cw.shape, const2),
            pl.BlockSpec(lbl.shape, const2),
            pl.BlockSpec(wn.shape, const2),
            pl.BlockSpec((None, grp, M_HEADS, HEAD_DIM, HEAD_DIM), st5),
            pl.BlockSpec((None, grp, M_HEADS, HEAD_DIM), st4),
            pl.BlockSpec((None, grp, 1, M_HEADS), st4),
            pl.BlockSpec((None, grp, G_HEADS, HEAD_DIM, HEAD_DIM), st5),
            pl.BlockSpec((None, grp, SUBLANES, 3 * G_W), st4),
            pl.BlockSpec((None, grp, H_HEADS, HEAD_DIM, HEAD_DIM), st5),
        ],
        out_specs=(
            pl.BlockSpec((grp, c, D_MIX), lambda b, n: (b, n, 0)),
            pl.BlockSpec((grp, M_HEADS, HEAD_DIM, HEAD_DIM), o4),
            pl.BlockSpec((grp, M_HEADS, HEAD_DIM), o3),
            pl.BlockSpec((grp, 1, M_HEADS), o3),
            pl.BlockSpec((grp, G_HEADS, HEAD_DIM, HEAD_DIM), o4),
            pl.BlockSpec((grp, SUBLANES, 3 * G_W), o3),
            pl.BlockSpec((grp, H_HEADS, HEAD_DIM, HEAD_DIM), o4),
        ),
        out_shape=out_shapes,
        scratch_shapes=[pltpu.VMEM((grp, SUBLANES + c, 3 * G_W), F32)],
        compiler_params=pltpu.CompilerParams(
            dimension_semantics=("parallel", "arbitrary"), vmem_limit_bytes=VMEM_LIMIT_BYTES),
        name="mixer",
    )(z, prm, cw, lbl, wn, st_c, st_n, st_m, st_s, st_cv, st_h)


def _permute_in_proj(w_in, b_in):
    o = 0
    pieces = {}
    for name, width in (("mq", M_W), ("mk", M_W), ("mv", M_W), ("mo", M_W), ("mi", M_HEADS), ("mf", M_HEADS),
                        ("gqkv", 3 * G_W), ("gg", G_W), ("gb", G_HEADS), ("ga", G_HEADS),
                        ("hq", H_W), ("hf", H_W), ("hi", H_W), ("hg", H_W)):
        pieces[name] = (o, o + width)
        o += width
    order = ("mq", "mk", "mv", "mo", "gqkv", "gg", "hq", "hf", "hi", "hg", "mi", "mf", "gb", "ga")

    def perm(a):
        cols = [a[..., pieces[n][0]:pieces[n][1]] for n in order]
        pad = jnp.zeros(a.shape[:-1] + (LANES - GATE_END,), a.dtype)
        return jnp.concatenate(cols + [pad], axis=-1)

    return perm(w_in).astype(BF16), perm(b_in).reshape(DEPTH, 1, Z_W)


def _gate_params(mlstm_f_bias, gdn_A_log, gdn_dt_bias):
    prm = jnp.zeros((DEPTH, SUBLANES, LANES), F32)
    prm = prm.at[:, 0, GATE_MF:GATE_MF + M_HEADS].set(mlstm_f_bias.astype(F32))
    prm = prm.at[:, 0, GATE_GA:GATE_GA + G_HEADS].set(gdn_dt_bias.astype(F32))
    prm = prm.at[:, 1, GATE_GA:GATE_GA + G_HEADS].set(gdn_A_log.astype(F32))
    return prm


def _trunk(x, mod, states, st_is_zero, params, c, sb, grp, in_bt, ffn_bt):
    (npre_mix, npost_mix, w_in, b_in, prm, conv_w, lb_logits, out_norm, w_out, npre_ffn, npost_ffn, w_ff1,
     w_ff2) = params
    nb, seq, _ = x.shape
    st_c, st_n, st_m, st_s, st_cv, st_h = states
    new = ([], [], [], [], [], [])
    for l in range(DEPTH):
        z = _inproj_call(x, mod[l], npre_mix[l], w_in[l], b_in[l], *in_bt)
        outs = _mixer_call(l, z.reshape(nb, seq, Z_W), prm[l], conv_w[l], lb_logits, out_norm[l],
                           st_c, st_n, st_m, st_s, st_cv, st_h, 0 if st_is_zero else l, c, sb, grp)
        ymix = outs[0].reshape(nb * seq, D_MIX)
        x = _ffn_call(x, ymix, mod[l], w_out[l], w_ff1[l], w_ff2[l], npost_mix[l], npre_ffn[l], npost_ffn[l],
                      *ffn_bt)
        for lst, s in zip(new, outs[1:]):
            lst.append(s)
    c1, n1, m1, s1, cv1, h1 = (jnp.stack(s) for s in new)
    return x, (c1, n1, m1.reshape(DEPTH, nb, M_HEADS), s1, cv1[:, :, SUBLANES - (CONV_W - 1):, :], h1)


def kernel(x_prompt, x_sample, state_mlstm_C, state_mlstm_n, state_mlstm_m, state_gdn_S, state_gdn_conv,
           state_hgrn_S, c_prompt, c_sample, w_ada, b_ada, norm_pre_mix, norm_post_mix, w_in, b_in,
           mlstm_f_bias, gdn_conv_w, gdn_A_log, gdn_dt_bias, hgrn_lb_logits, mix_out_norm, w_out,
           norm_pre_ffn, norm_post_ffn, w_ff1, w_ff2):
    bp, seq_p, _ = x_prompt.shape
    bs, seq_s, _ = x_sample.shape

    mod = _ada_call(jnp.concatenate([c_prompt, c_sample], axis=0), w_ada, b_ada)
    mod_p = mod[:, :bp].reshape(DEPTH, bp, 1, 6 * D_MODEL)
    mod_s = mod[:, bp:].reshape(DEPTH, bs, 1, 6 * D_MODEL)

    w_in_p, b_in_p = _permute_in_proj(w_in, b_in)
    row = lambda a: a.reshape(DEPTH, 1, a.shape[-1]).astype(F32)
    params = (row(norm_pre_mix), row(norm_post_mix), w_in_p, b_in_p,
              _gate_params(mlstm_f_bias, gdn_A_log, gdn_dt_bias), gdn_conv_w.astype(F32),
              hgrn_lb_logits.astype(F32), row(mix_out_norm), w_out.astype(BF16), row(norm_pre_ffn),
              row(norm_post_ffn), w_ff1.astype(BF16), w_ff2.astype(BF16))

    pad_conv = ((0, 0), (0, 0), (SUBLANES - (CONV_W - 1), 0), (0, 0))
    zeros = (jnp.zeros((1, bp, M_HEADS, HEAD_DIM, HEAD_DIM), F32),
             jnp.zeros((1, bp, M_HEADS, HEAD_DIM), F32),
             jnp.zeros((1, bp, 1, M_HEADS), F32),
             jnp.zeros((1, bp, G_HEADS, HEAD_DIM, HEAD_DIM), F32),
             jnp.zeros((1, bp, SUBLANES, 3 * G_W), F32),
             jnp.zeros((1, bp, H_HEADS, HEAD_DIM, HEAD_DIM), F32))
    past = (state_mlstm_C.astype(F32), state_mlstm_n.astype(F32),
            state_mlstm_m.astype(F32).reshape(DEPTH, bs, 1, M_HEADS), state_gdn_S.astype(F32),
            jnp.pad(state_gdn_conv.astype(F32), pad_conv), state_hgrn_S.astype(F32))

    y_p, st_p = _trunk(x_prompt, mod_p, zeros, True, params, PROMPT_CHUNK, HGRN_SUBBLOCK, PROMPT_GROUP,
                       (1, IN_TILE), (1, FFN_TILE))
    y_s, st_s = _trunk(x_sample, mod_s, past, False, params, seq_s, min(seq_s, HGRN_SUBBLOCK), SAMPLE_GROUP,
                       (IN_TILE // seq_s, seq_s), (FFN_TILE // seq_s, seq_s))
    return (y_p, y_s) + st_p + st_s
```

```python
import functools

import jax
import jax.numpy as jnp
from jax import lax
from jax.experimental import pallas as pl
from jax.experimental.pallas import tpu as pltpu

F32 = jnp.float32
BF16 = jnp.bfloat16

D_MODEL = 1024
DEPTH = 4
HEAD_DIM = 64
M_HEADS = 6
G_HEADS = 6
H_HEADS = 4
M_W = M_HEADS * HEAD_DIM
G_W = G_HEADS * HEAD_DIM
H_W = H_HEADS * HEAD_DIM
D_MIX = M_W + G_W + H_W
CONV_W = 4
D_FF = 4 * D_MODEL
EPS = 1e-6
QK_SCALE = HEAD_DIM ** -0.5

LANES = 128
SUBLANES = 8
VMEM_LIMIT_BYTES = 56 * 1024 * 1024

OFF_MQ = 0
OFF_MK = OFF_MQ + M_W
OFF_MV = OFF_MK + M_W
OFF_MO = OFF_MV + M_W
OFF_GQKV = OFF_MO + M_W
OFF_GG = OFF_GQKV + 3 * G_W
OFF_HQ = OFF_GG + G_W
OFF_HF = OFF_HQ + H_W
OFF_HI = OFF_HF + H_W
OFF_HG = OFF_HI + H_W
OFF_GATE = OFF_HG + H_W
Z_W = OFF_GATE + LANES
GATE_MI = 0
GATE_MF = GATE_MI + M_HEADS
GATE_GB = GATE_MF + M_HEADS
GATE_GA = GATE_GB + G_HEADS
GATE_END = GATE_GA + G_HEADS

PROMPT_CHUNK = 64
HGRN_SUBBLOCK = 16
PROMPT_GROUP = 4
SAMPLE_GROUP = 16
IN_TILE = 512
FFN_TILE = 512
FF_BLOCK = 1024
ADA_BLOCK = 1536


def _mm(a, b):
    return jnp.dot(a.astype(BF16), b.astype(BF16), preferred_element_type=F32)


def _mm_f32(a, b):
    return jnp.dot(a, b, preferred_element_type=F32, precision=lax.Precision.HIGHEST)


def _bmm(a, b):
    return jnp.einsum("nmk,nkp->nmp", a.astype(BF16), b.astype(BF16), preferred_element_type=F32)


def _bmm_nt(a, b):
    return jnp.einsum("nmk,npk->nmp", a.astype(BF16), b.astype(BF16), preferred_element_type=F32)


def _bmm_tn(a, b):
    return jnp.einsum("nsk,nsp->nkp", a.astype(BF16), b.astype(BF16), preferred_element_type=F32)


def _sigmoid(x):
    return 1.0 / (1.0 + jnp.exp(-x))


def _log1pexp_negabs(x):
    return jnp.log1p(jnp.exp(-jnp.abs(x)))


def _rms(x, w):
    ms = jnp.mean(x * x, axis=-1, keepdims=True)
    return x * lax.rsqrt(ms + EPS) * w


def _ada_kernel(c_ref, w_ref, b_ref, o_ref):
    mod = jnp.dot(c_ref[...].astype(BF16), w_ref[...].astype(BF16), preferred_element_type=F32) + b_ref[...]
    o_ref[...] = mod[:, None, :]


def _ada_call(c_all, w_ada, b_ada):
    rows = c_all.shape[0]
    nblk = (6 * D_MODEL) // ADA_BLOCK
    return pl.pallas_call(
        _ada_kernel,
        grid=(DEPTH, nblk),
        in_specs=[
            pl.BlockSpec((rows, D_MODEL), lambda l, j: (0, 0)),
            pl.BlockSpec((None, D_MODEL, ADA_BLOCK), lambda l, j: (l, 0, j)),
            pl.BlockSpec((None, 1, ADA_BLOCK), lambda l, j: (l, 0, j)),
        ],
        out_specs=pl.BlockSpec((None, rows, 1, ADA_BLOCK), lambda l, j: (l, 0, 0, j)),
        out_shape=jax.ShapeDtypeStruct((DEPTH, rows, 1, 6 * D_MODEL), F32),
        compiler_params=pltpu.CompilerParams(
            dimension_semantics=("parallel", "parallel"), vmem_limit_bytes=VMEM_LIMIT_BYTES),
        name="adaln",
    )(c_all, w_ada, b_ada.reshape(DEPTH, 1, 6 * D_MODEL))


def _inproj_kernel(x_ref, mod_ref, npre_ref, w_ref, b_ref, z_ref):
    bb, t, _ = x_ref.shape
    mod = mod_ref[...]
    sh1 = mod[:, :, 0:D_MODEL]
    sc1 = mod[:, :, D_MODEL:2 * D_MODEL]
    h = _rms(x_ref[...], npre_ref[...]) * (1.0 + sc1) + sh1
    hb = h.reshape(bb * t, D_MODEL).astype(BF16)
    z_ref[...] = jnp.dot(hb, w_ref[...], preferred_element_type=F32) + b_ref[...]


def _inproj_call(x, mod, layer, mod_row0, npre, w, b, bb, t):
    nb, seq, _ = x.shape
    grid = (nb // bb, seq // t)
    nseq = seq // t
    mod_blk0 = mod_row0 // bb
    return pl.pallas_call(
        _inproj_kernel,
        grid=grid,
        in_specs=[
            pl.BlockSpec((bb, t, D_MODEL), lambda i, j: (i, j, 0)),
            pl.BlockSpec((None, bb, 1, 6 * D_MODEL), lambda i, j: (layer, mod_blk0 + i, 0, 0)),
            pl.BlockSpec((1, D_MODEL), lambda i, j: (0, 0)),
            pl.BlockSpec((D_MODEL, Z_W), lambda i, j: (0, 0)),
            pl.BlockSpec((1, Z_W), lambda i, j: (0, 0)),
        ],
        out_specs=pl.BlockSpec((bb * t, Z_W), lambda i, j: (i * nseq + j, 0)),
        out_shape=jax.ShapeDtypeStruct((nb * seq, Z_W), F32),
        compiler_params=pltpu.CompilerParams(
            dimension_semantics=("parallel", "parallel"), vmem_limit_bytes=VMEM_LIMIT_BYTES),
        name="inproj",
    )(x, mod, npre, w, b)


def _ffn_kernel(x_ref, ym_ref, mod_ref, wout_ref, w1_ref, w2_ref, npost_ref, npre_ref, npostf_ref,
                o_ref, x1_s, h2_s, acc_s):
    j = pl.program_id(2)
    bb, t, _ = x_ref.shape

    @pl.when(j == 0)
    def _():
        mod = mod_ref[...]
        g1 = mod[:, :, 2 * D_MODEL:3 * D_MODEL]
        sh2 = mod[:, :, 3 * D_MODEL:4 * D_MODEL]
        sc2 = mod[:, :, 4 * D_MODEL:5 * D_MODEL]
        y = jnp.dot(ym_ref[...].astype(BF16), wout_ref[...], preferred_element_type=F32)
        x1 = x_ref[...] + g1 * _rms(y.reshape(bb, t, D_MODEL), npost_ref[...])
        x1_s[...] = x1
        h2 = _rms(x1, npre_ref[...]) * (1.0 + sc2) + sh2
        h2_s[...] = h2.reshape(bb * t, D_MODEL).astype(BF16)
        acc_s[...] = jnp.zeros_like(acc_s)

    a = jnp.dot(h2_s[...], w1_ref[...], preferred_element_type=F32)
    a = jnp.square(jnp.maximum(a, 0.0))
    acc_s[...] += jnp.dot(a.astype(BF16), w2_ref[...], preferred_element_type=F32)

    @pl.when(j == pl.num_programs(2) - 1)
    def _():
        g2 = mod_ref[...][:, :, 5 * D_MODEL:6 * D_MODEL]
        f = acc_s[...].reshape(bb, t, D_MODEL)
        o_ref[...] = x1_s[...] + g2 * _rms(f, npostf_ref[...])


def _ffn_call(x, ymix, mod, layer, mod_row0, wout, w1, w2, npost, npre, npostf, bb, t):
    nb, seq, _ = x.shape
    nseq = seq // t
    nff = D_FF // FF_BLOCK
    mod_blk0 = mod_row0 // bb
    return pl.pallas_call(
        _ffn_kernel,
        grid=(nb // bb, nseq, nff),
        in_specs=[
            pl.BlockSpec((bb, t, D_MODEL), lambda i, s, j: (i, s, 0)),
            pl.BlockSpec((bb * t, D_MIX), lambda i, s, j: (i * nseq + s, 0)),
            pl.BlockSpec((None, bb, 1, 6 * D_MODEL), lambda i, s, j: (layer, mod_blk0 + i, 0, 0)),
            pl.BlockSpec((D_MIX, D_MODEL), lambda i, s, j: (0, 0)),
            pl.BlockSpec((D_MODEL, FF_BLOCK), lambda i, s, j: (0, j)),
            pl.BlockSpec((FF_BLOCK, D_MODEL), lambda i, s, j: (j, 0)),
            pl.BlockSpec((1, D_MODEL), lambda i, s, j: (0, 0)),
            pl.BlockSpec((1, D_MODEL), lambda i, s, j: (0, 0)),
            pl.BlockSpec((1, D_MODEL), lambda i, s, j: (0, 0)),
        ],
        out_specs=pl.BlockSpec((bb, t, D_MODEL), lambda i, s, j: (i, s, 0)),
        out_shape=jax.ShapeDtypeStruct(x.shape, F32),
        scratch_shapes=[
            pltpu.VMEM((bb, t, D_MODEL), F32),
            pltpu.VMEM((bb * t, D_MODEL), BF16),
            pltpu.VMEM((bb * t, D_MODEL), F32),
        ],
        compiler_params=pltpu.CompilerParams(
            dimension_semantics=("parallel", "parallel", "arbitrary"), vmem_limit_bytes=VMEM_LIMIT_BYTES),
        name="outffn",
    )(x, ymix, mod, wout, w1, w2, npost, npre, npostf)


def _unit_lower_inverse(lm, eye, row, col, c):
    t = None
    s = 1
    while s < c:
        same = (row // (2 * s)) == (col // (2 * s))
        off = same & ((row % (2 * s)) >= s) & ((col % (2 * s)) < s)
        coff = jnp.where(off, lm, 0.0)
        if t is None:
            t = eye - coff
        else:
            tb = t.astype(BF16)
            t = t - _bmm(_bmm(tb, coff), tb)
        s *= 2
    return t


def _head_rmsnorm_gate(o, wn, gate):
    ms = jnp.mean(o * o, axis=2, keepdims=True)
    return o * lax.rsqrt(ms + EPS) * wn * gate


def _mixer_kernel(layer, c, sb, grp,
                  z_ref, prm_ref, cw_ref, lbl_ref, wn_ref, c0_ref, n0_ref, m0_ref, s0_ref, cv0_ref, h0_ref,
                  acc_c, acc_n, acc_m, acc_s, acc_cv, acc_h,
                  y_ref, c1_ref, n1_ref, m1_ref, s1_ref, cv1_ref, h1_ref, xbuf):
    del acc_c, acc_n, acc_m, acc_s, acc_cv, acc_h
    step = pl.program_id(1)
    first = SUBLANES - (CONV_W - 1)

    @pl.when(step == 0)
    def _():
        c1_ref[...] = c0_ref[...]
        n1_ref[...] = n0_ref[...]
        m1_ref[...] = m0_ref[...]
        s1_ref[...] = s0_ref[...]
        h1_ref[...] = h0_ref[...]
        xbuf[:, first:SUBLANES, :] = cv0_ref[...]

    row = lax.broadcasted_iota(jnp.int32, (c, c), 0)
    col = lax.broadcasted_iota(jnp.int32, (c, c), 1)
    incl = row >= col
    strict = row > col
    eye_c = (row == col).astype(F32)
    tri = incl.astype(F32)
    r64 = lax.broadcasted_iota(jnp.int32, (HEAD_DIM, HEAD_DIM), 0)
    c64 = lax.broadcasted_iota(jnp.int32, (HEAD_DIM, HEAD_DIM), 1)
    eye_d = (r64 == c64).astype(F32)

    def heads_of(ref_or_val, off, nh):
        return jnp.concatenate(
            [ref_or_val[:, :, off + h * HEAD_DIM:off + (h + 1) * HEAD_DIM] for h in range(nh)], axis=0)

    def cols_of(a, off, nh):
        return jnp.concatenate([a[:, :, off + h:off + h + 1] for h in range(nh)], axis=0)

    def rows_of(a_t, off, nh):
        return jnp.concatenate([a_t[:, off + h:off + h + 1, :] for h in range(nh)], axis=0)

    def norm_weights(off, nh):
        return jnp.concatenate(
            [jnp.broadcast_to(wn_ref[:, off + h * HEAD_DIM:off + (h + 1) * HEAD_DIM][None], (grp, 1, HEAD_DIM))
             for h in range(nh)], axis=0)

    def store_heads(val, off, nh):
        for h in range(nh):
            y_ref[:, :, off + h * HEAD_DIM:off + (h + 1) * HEAD_DIM] = val[h * grp:(h + 1) * grp]

    zg = z_ref[:, :, OFF_GATE:OFF_GATE + LANES]
    lane = lax.broadcasted_iota(jnp.int32, (1, 1, LANES), 2)
    tg = zg + prm_ref[0:1, :]
    tail = _log1pexp_negabs(tg)
    logsig = -(jnp.maximum(-tg, 0.0) + tail)
    splus = jnp.maximum(tg, 0.0) + tail
    lgv = -jnp.exp(prm_ref[1:2, :]) * splus
    gates = jnp.where(lane < GATE_MF, zg,
                      jnp.where(lane < GATE_GB, logsig,
                                jnp.where(lane < GATE_GA, _sigmoid(tg),
                                          jnp.where(lane < GATE_END, lgv, 0.0))))
    cs = jnp.stack([_mm_f32(tri, gates[g]) for g in range(grp)])
    gates_t = jnp.stack([gates[g].T for g in range(grp)])
    cs_t = jnp.stack([cs[g].T for g in range(grp)])

    q = heads_of(z_ref, OFF_MQ, M_HEADS)
    k = heads_of(z_ref, OFF_MK, M_HEADS) * QK_SCALE
    v = heads_of(z_ref, OFF_MV, M_HEADS)
    ig_row = rows_of(gates_t, GATE_MI, M_HEADS)
    ig_col = cols_of(gates, GATE_MI, M_HEADS)
    b_col = cols_of(cs, GATE_MF, M_HEADS)
    b_row = rows_of(cs_t, GATE_MF, M_HEADS)
    m_prev = jnp.concatenate([m1_ref[:, :, h:h + 1] for h in range(M_HEADS)], axis=0)
    dmat = jnp.where(incl, b_col - b_row + ig_row, -jnp.inf)
    inter = b_col + m_prev
    m_t = jnp.maximum(inter, jnp.max(dmat, axis=2, keepdims=True))
    w = jnp.exp(dmat - m_t)
    wi = jnp.exp(inter - m_t)
    wqk = w * _bmm_nt(q, k)
    c_st = jnp.concatenate([c1_ref[:, h] for h in range(M_HEADS)], axis=0)
    n_st = jnp.concatenate([n1_ref[:, h:h + 1, :] for h in range(M_HEADS)], axis=0)
    num = _bmm(wqk, v) + wi * _bmm(q, c_st)
    den = jnp.sum(wqk, axis=2, keepdims=True) + wi * jnp.sum(q * n_st, axis=2, keepdims=True)
    hm = num / jnp.maximum(jnp.abs(den), 1.0)
    m_new = m_t[:, c - 1:c, :]
    b_last = b_col[:, c - 1:c, :]
    ws = jnp.exp(b_last - b_col + ig_col - m_new)
    decay = jnp.exp(b_last + m_prev - m_new)
    kw = k * ws
    c_new = decay * c_st + _bmm_tn(kw, v)
    n_new = decay * n_st + jnp.sum(kw, axis=1, keepdims=True)
    for h in range(M_HEADS):
        c1_ref[:, h] = c_new[h * grp:(h + 1) * grp]
        n1_ref[:, h:h + 1, :] = n_new[h * grp:(h + 1) * grp]
        m1_ref[:, :, h:h + 1] = m_new[h * grp:(h + 1) * grp]
    store_heads(_head_rmsnorm_gate(hm, norm_weights(0, M_HEADS), _sigmoid(heads_of(z_ref, OFF_MO, M_HEADS))),
                0, M_HEADS)

    xbuf[:, SUBLANES:SUBLANES + c, :] = z_ref[:, :, OFF_GQKV:OFF_GQKV + 3 * G_W]
    pre = cw_ref[0:1, :] * xbuf[:, first:first + c, :]
    for j in range(1, CONV_W):
        pre = pre + cw_ref[j:j + 1, :] * xbuf[:, first + j:first + j + c, :]
    conv = pre * _sigmoid(pre)
    cv1_ref[...] = xbuf[:, c + first:c + SUBLANES, :]
    xbuf[:, 0:SUBLANES, :] = xbuf[:, c:c + SUBLANES, :]

    gq = heads_of(conv, 0, G_HEADS)
    gk = heads_of(conv, G_W, G_HEADS)
    gv = heads_of(conv, 2 * G_W, G_HEADS)
    gq = gq * lax.rsqrt(jnp.sum(gq * gq, axis=2, keepdims=True) + EPS) * QK_SCALE
    gk = gk * lax.rsqrt(jnp.sum(gk * gk, axis=2, keepdims=True) + EPS)
    beta = cols_of(gates, GATE_GB, G_HEADS)
    g_col = cols_of(cs, GATE_GA, G_HEADS)
    g_row = rows_of(cs_t, GATE_GA, G_HEADS)
    ediff = jnp.exp(jnp.where(incl, g_col - g_row, 0.0))
    kb = gk * beta
    lm = jnp.where(strict, _bmm_nt(kb, gk) * ediff, 0.0)
    tinv = _unit_lower_inverse(lm, eye_c, row, col, c)
    eg = jnp.exp(g_col)
    u = _bmm(tinv, gv * beta)
    wk = _bmm(tinv, kb * eg)
    s_st = jnp.concatenate([s1_ref[:, h] for h in range(G_HEADS)], axis=0)
    v_new = u - _bmm(wk, s_st)
    attn = jnp.where(incl, _bmm_nt(gq, gk) * ediff, 0.0)
    o = _bmm(gq * eg, s_st) + _bmm(attn, v_new)
    gl = g_col[:, c - 1:c, :]
    s_new = s_st * jnp.exp(gl) + _bmm_tn(gk * jnp.exp(gl - g_col), v_new)
    for h in range(G_HEADS):
        s1_ref[:, h] = s_new[h * grp:(h + 1) * grp]
    gg = heads_of(z_ref, OFF_GG, G_HEADS)
    store_heads(_head_rmsnorm_gate(o, norm_weights(M_W, G_HEADS), gg * _sigmoid(gg)), M_W, G_HEADS)

    hf = z_ref[:, :, OFF_HF:OFF_HF + H_W]
    ls = -(jnp.maximum(-hf, 0.0) + _log1pexp_negabs(hf))
    if layer == 0:
        lf = ls
        kk = _sigmoid(-hf)
    else:
        lg = lbl_ref[...]
        ex = jnp.exp(lg - jnp.max(lg, axis=0, keepdims=True))
        sm = ex / jnp.sum(ex, axis=0, keepdims=True)
        cum = sm[0:1, :]
        for j in range(1, layer + 1):
            cum = cum + sm[j:j + 1, :]
        lb = cum - sm[0:1, :]
        a = jnp.log(lb)
        b = jnp.log1p(-lb) + ls
        lf = jnp.maximum(a, b) + _log1pexp_negabs(a - b)
        kk = (1.0 - lb) * _sigmoid(-hf)
    gcum = jnp.stack([_mm_f32(tri, lf[g]) for g in range(grp)])
    nh = H_HEADS * grp
    ones_d = jnp.ones((HEAD_DIM, HEAD_DIM), F32)
    trow = lax.broadcasted_iota(jnp.int32, (sb, HEAD_DIM), 0)
    q = heads_of(z_ref, OFF_HQ, H_HEADS)
    k = heads_of(kk, 0, H_HEADS)
    v = heads_of(z_ref, OFF_HI, H_HEADS)
    gh = heads_of(gcum, 0, H_HEADS)
    s_st = jnp.concatenate([h1_ref[:, h] for h in range(H_HEADS)], axis=0)
    inter = _bmm(q * jnp.exp(gh), s_st)
    blocks = []
    for blk in range(c // sb):
        r0 = blk * sb
        qi = q[:, r0:r0 + sb]
        ki = k[:, r0:r0 + sb]
        vi = v[:, r0:r0 + sb]
        gi = gh[:, r0:r0 + sb]
        prods = []
        for s in range(sb):
            e = jnp.exp(jnp.where(trow >= s, gi - gi[:, s:s + 1], -jnp.inf))
            prods.append(qi * ki[:, s:s + 1] * e)
        stacked = jnp.concatenate(prods, axis=1).reshape(nh * sb * sb, HEAD_DIM)
        rs = _mm(stacked, ones_d).reshape(nh, sb * sb, HEAD_DIM)
        oi = rs[:, 0:sb] * vi[:, 0:1]
        for s in range(1, sb):
            oi = oi + rs[:, s * sb:(s + 1) * sb] * vi[:, s:s + 1]
        if blk > 0:
            ref = gh[:, r0:r0 + 1]
            qt = qi * jnp.exp(gi - ref)
            kt = k[:, 0:r0] * jnp.exp(ref - gh[:, 0:r0])
            oi = oi + _bmm(_bmm_nt(qt, kt), v[:, 0:r0])
        blocks.append(oi)
    o = inter + (blocks[0] if len(blocks) == 1 else jnp.concatenate(blocks, axis=1))
    gl = gh[:, c - 1:c]
    decay_col = jnp.sum(eye_d * jnp.exp(gl), axis=2, keepdims=True)
    s_new = decay_col * s_st + _bmm_tn(k * jnp.exp(gl - gh), v)
    for h in range(H_HEADS):
        h1_ref[:, h] = s_new[h * grp:(h + 1) * grp]
    hg = heads_of(z_ref, OFF_HG, H_HEADS)
    store_heads(_head_rmsnorm_gate(o, norm_weights(M_W + G_W, H_HEADS), hg * _sigmoid(hg)), M_W + G_W, H_HEADS)


def _mixer_call(layer, z, prm, cw, lbl, wn, init, init_layer, acc, c, sb, grp):
    nb, seq, _ = z.shape
    nchunks = seq // c
    kern = functools.partial(_mixer_kernel, layer, c, sb, grp)
    const2 = lambda b, n: (0, 0)
    st5 = lambda b, n: (init_layer, b, 0, 0, 0)
    st4 = lambda b, n: (init_layer, b, 0, 0)
    o5 = lambda b, n: (layer, b, 0, 0, 0)
    o4 = lambda b, n: (layer, b, 0, 0)
    state_blocks = ((grp, M_HEADS, HEAD_DIM, HEAD_DIM), (grp, M_HEADS, HEAD_DIM), (grp, 1, M_HEADS),
                    (grp, G_HEADS, HEAD_DIM, HEAD_DIM), (grp, CONV_W - 1, 3 * G_W),
                    (grp, H_HEADS, HEAD_DIM, HEAD_DIM))
    n_in = 5 + len(state_blocks)
    return pl.pallas_call(
        kern,
        grid=(nb // grp, nchunks),
        in_specs=[
            pl.BlockSpec((grp, c, Z_W), lambda b, n: (b, n, 0)),
            pl.BlockSpec(prm.shape, const2),
            pl.BlockSpec(cw.shape, const2),
            pl.BlockSpec(lbl.shape, const2),
            pl.BlockSpec(wn.shape, const2),
        ] + [pl.BlockSpec((None,) + blk, st5 if len(blk) == 4 else st4) for blk in state_blocks]
          + [pl.BlockSpec(memory_space=pl.ANY) for _ in state_blocks],
        out_specs=[pl.BlockSpec((grp, c, D_MIX), lambda b, n: (b, n, 0))]
                  + [pl.BlockSpec((None,) + blk, o5 if len(blk) == 4 else o4) for blk in state_blocks],
        out_shape=[jax.ShapeDtypeStruct((nb, seq, D_MIX), F32)]
                  + [jax.ShapeDtypeStruct(a.shape, F32) for a in acc],
        input_output_aliases={n_in + i: 1 + i for i in range(len(state_blocks))},
        scratch_shapes=[pltpu.VMEM((grp, SUBLANES + c, 3 * G_W), F32)],
        compiler_params=pltpu.CompilerParams(
            dimension_semantics=("parallel", "arbitrary"), vmem_limit_bytes=VMEM_LIMIT_BYTES),
        name="mixer",
    )(z, prm, cw, lbl, wn, *init, *acc)


def _permute_in_proj(w_in, b_in):
    o = 0
    pieces = {}
    for name, width in (("mq", M_W), ("mk", M_W), ("mv", M_W), ("mo", M_W), ("mi", M_HEADS), ("mf", M_HEADS),
                        ("gqkv", 3 * G_W), ("gg", G_W), ("gb", G_HEADS), ("ga", G_HEADS),
                        ("hq", H_W), ("hf", H_W), ("hi", H_W), ("hg", H_W)):
        pieces[name] = (o, o + width)
        o += width
    order = ("mq", "mk", "mv", "mo", "gqkv", "gg", "hq", "hf", "hi", "hg", "mi", "mf", "gb", "ga")

    def perm(a):
        cols = [a[..., pieces[n][0]:pieces[n][1]] for n in order]
        pad = jnp.zeros(a.shape[:-1] + (LANES - GATE_END,), a.dtype)
        return jnp.concatenate(cols + [pad], axis=-1)

    return perm(w_in).astype(BF16), perm(b_in).reshape(DEPTH, 1, Z_W)


def _gate_params(mlstm_f_bias, gdn_A_log, gdn_dt_bias):
    prm = jnp.zeros((DEPTH, SUBLANES, LANES), F32)
    prm = prm.at[:, 0, GATE_MF:GATE_MF + M_HEADS].set(mlstm_f_bias.astype(F32))
    prm = prm.at[:, 0, GATE_GA:GATE_GA + G_HEADS].set(gdn_dt_bias.astype(F32))
    prm = prm.at[:, 1, GATE_GA:GATE_GA + G_HEADS].set(gdn_A_log.astype(F32))
    return prm


def _trunk(x, mod, mod_row0, init, init_is_shared, params, c, sb, grp, in_bt, ffn_bt):
    (npre_mix, npost_mix, w_in, b_in, prm, conv_w, lb_logits, out_norm, w_out, npre_ffn, npost_ffn, w_ff1,
     w_ff2) = params
    nb, seq, _ = x.shape
    acc = tuple(jnp.zeros((DEPTH,) + a.shape[1:], F32) for a in init)
    for l in range(DEPTH):
        z = _inproj_call(x, mod, l, mod_row0, npre_mix[l], w_in[l], b_in[l], *in_bt)
        outs = _mixer_call(l, z.reshape(nb, seq, Z_W), prm[l], conv_w[l], lb_logits, out_norm[l],
                           init, 0 if init_is_shared else l, acc, c, sb, grp)
        ymix = outs[0].reshape(nb * seq, D_MIX)
        acc = tuple(outs[1:])
        x = _ffn_call(x, ymix, mod, l, mod_row0, w_out[l], w_ff1[l], w_ff2[l], npost_mix[l], npre_ffn[l],
                      npost_ffn[l], *ffn_bt)
    c1, n1, m1, s1, cv1, h1 = acc
    return x, (c1, n1, m1.reshape(DEPTH, nb, M_HEADS), s1, cv1, h1)


def kernel(x_prompt, x_sample, state_mlstm_C, state_mlstm_n, state_mlstm_m, state_gdn_S, state_gdn_conv,
           state_hgrn_S, c_prompt, c_sample, w_ada, b_ada, norm_pre_mix, norm_post_mix, w_in, b_in,
           mlstm_f_bias, gdn_conv_w, gdn_A_log, gdn_dt_bias, hgrn_lb_logits, mix_out_norm, w_out,
           norm_pre_ffn, norm_post_ffn, w_ff1, w_ff2):
    bp, seq_p, _ = x_prompt.shape
    bs, seq_s, _ = x_sample.shape

    mod = _ada_call(jnp.concatenate([c_sample, c_prompt], axis=0), w_ada, b_ada)

    w_in_p, b_in_p = _permute_in_proj(w_in, b_in)
    row = lambda a: a.reshape(DEPTH, 1, a.shape[-1]).astype(F32)
    params = (row(norm_pre_mix), row(norm_post_mix), w_in_p, b_in_p,
              _gate_params(mlstm_f_bias, gdn_A_log, gdn_dt_bias), gdn_conv_w.astype(F32),
              hgrn_lb_logits.astype(F32), row(mix_out_norm), w_out.astype(BF16), row(norm_pre_ffn),
              row(norm_post_ffn), w_ff1.astype(BF16), w_ff2.astype(BF16))

    zeros = (jnp.zeros((1, bp, M_HEADS, HEAD_DIM, HEAD_DIM), F32),
             jnp.zeros((1, bp, M_HEADS, HEAD_DIM), F32),
             jnp.zeros((1, bp, 1, M_HEADS), F32),
             jnp.zeros((1, bp, G_HEADS, HEAD_DIM, HEAD_DIM), F32),
             jnp.zeros((1, bp, CONV_W - 1, 3 * G_W), F32),
             jnp.zeros((1, bp, H_HEADS, HEAD_DIM, HEAD_DIM), F32))
    past = (state_mlstm_C.astype(F32), state_mlstm_n.astype(F32),
            state_mlstm_m.astype(F32).reshape(DEPTH, bs, 1, M_HEADS), state_gdn_S.astype(F32),
            state_gdn_conv.astype(F32), state_hgrn_S.astype(F32))

    y_p, st_p = _trunk(x_prompt, mod, bs, zeros, True, params, PROMPT_CHUNK, HGRN_SUBBLOCK, PROMPT_GROUP,
                       (1, IN_TILE), (1, FFN_TILE))
    y_s, st_s = _trunk(x_sample, mod, 0, past, False, params, seq_s, min(seq_s, HGRN_SUBBLOCK), SAMPLE_GROUP,
                       (IN_TILE // seq_s, seq_s), (FFN_TILE // seq_s, seq_s))
    return (y_p, y_s) + st_p + st_s
```

```python
import functools

import jax
import jax.numpy as jnp
from jax import lax
from jax.experimental import pallas as pl
from jax.experimental.pallas import tpu as pltpu

F32 = jnp.float32
BF16 = jnp.bfloat16

D_MODEL = 1024
DEPTH = 4
HEAD_DIM = 64
M_HEADS = 6
G_HEADS = 6
H_HEADS = 4
M_W = M_HEADS * HEAD_DIM
G_W = G_HEADS * HEAD_DIM
H_W = H_HEADS * HEAD_DIM
D_MIX = M_W + G_W + H_W
CONV_W = 4
D_FF = 4 * D_MODEL
EPS = 1e-6
QK_SCALE = HEAD_DIM ** -0.5

LANES = 128
SUBLANES = 8
VMEM_LIMIT_BYTES = 56 * 1024 * 1024

OFF_MQ = 0
OFF_MK = OFF_MQ + M_W
OFF_MV = OFF_MK + M_W
OFF_MO = OFF_MV + M_W
OFF_GQKV = OFF_MO + M_W
OFF_GG = OFF_GQKV + 3 * G_W
OFF_HQ = OFF_GG + G_W
OFF_HF = OFF_HQ + H_W
OFF_HI = OFF_HF + H_W
OFF_HG = OFF_HI + H_W
OFF_GATE = OFF_HG + H_W
Z_W = OFF_GATE + LANES
GATE_MI = 0
GATE_MF = GATE_MI + M_HEADS
GATE_GB = GATE_MF + M_HEADS
GATE_GA = GATE_GB + G_HEADS
GATE_END = GATE_GA + G_HEADS

PROMPT_CHUNK = 64
HGRN_SUBBLOCK = 16
PROMPT_GROUP = 4
SAMPLE_GROUP = 16
IN_TILE = 512
FFN_TILE = 512
FF_BLOCK = 1024
ADA_BLOCK = 1536


def _mm(a, b):
    return jnp.dot(a.astype(BF16), b.astype(BF16), preferred_element_type=F32)


def _mm_f32(a, b):
    return jnp.dot(a, b, preferred_element_type=F32, precision=lax.Precision.HIGHEST)


def _bmm(a, b):
    return jnp.einsum("nmk,nkp->nmp", a.astype(BF16), b.astype(BF16), preferred_element_type=F32)


def _bmm_nt(a, b):
    return jnp.einsum("nmk,npk->nmp", a.astype(BF16), b.astype(BF16), preferred_element_type=F32)


def _bmm_tn(a, b):
    return jnp.einsum("nsk,nsp->nkp", a.astype(BF16), b.astype(BF16), preferred_element_type=F32)


def _sigmoid(x):
    return 1.0 / (1.0 + jnp.exp(-x))


def _log1pexp_negabs(x):
    return jnp.log1p(jnp.exp(-jnp.abs(x)))


def _rms(x, w):
    ms = jnp.mean(x * x, axis=-1, keepdims=True)
    return x * lax.rsqrt(ms + EPS) * w


def _ada_kernel(c_ref, w_ref, b_ref, o_ref):
    mod = jnp.dot(c_ref[...].astype(BF16), w_ref[...].astype(BF16), preferred_element_type=F32) + b_ref[...]
    o_ref[...] = mod[:, None, :]


def _ada_call(c_all, w_ada, b_ada):
    rows = c_all.shape[0]
    nblk = (6 * D_MODEL) // ADA_BLOCK
    return pl.pallas_call(
        _ada_kernel,
        grid=(DEPTH, nblk),
        in_specs=[
            pl.BlockSpec((rows, D_MODEL), lambda l, j: (0, 0)),
            pl.BlockSpec((None, D_MODEL, ADA_BLOCK), lambda l, j: (l, 0, j)),
            pl.BlockSpec((None, 1, ADA_BLOCK), lambda l, j: (l, 0, j)),
        ],
        out_specs=pl.BlockSpec((None, rows, 1, ADA_BLOCK), lambda l, j: (l, 0, 0, j)),
        out_shape=jax.ShapeDtypeStruct((DEPTH, rows, 1, 6 * D_MODEL), F32),
        compiler_params=pltpu.CompilerParams(
            dimension_semantics=("parallel", "parallel"), vmem_limit_bytes=VMEM_LIMIT_BYTES),
        name="adaln",
    )(c_all, w_ada, b_ada.reshape(DEPTH, 1, 6 * D_MODEL))


def _inproj_kernel(x_ref, mod_ref, npre_ref, w_ref, b_ref, z_ref):
    bb, t, _ = x_ref.shape
    mod = mod_ref[...]
    sh1 = mod[:, :, 0:D_MODEL]
    sc1 = mod[:, :, D_MODEL:2 * D_MODEL]
    h = _rms(x_ref[...], npre_ref[...]) * (1.0 + sc1) + sh1
    hb = h.reshape(bb * t, D_MODEL).astype(BF16)
    z_ref[...] = jnp.dot(hb, w_ref[...], preferred_element_type=F32) + b_ref[...]


def _inproj_call(x, mod, layer, mod_row0, npre, w, b, bb, t):
    nb, seq, _ = x.shape
    grid = (nb // bb, seq // t)
    nseq = seq // t
    mod_blk0 = mod_row0 // bb
    return pl.pallas_call(
        _inproj_kernel,
        grid=grid,
        in_specs=[
            pl.BlockSpec((bb, t, D_MODEL), lambda i, j: (i, j, 0)),
            pl.BlockSpec((None, bb, 1, 6 * D_MODEL), lambda i, j: (layer, mod_blk0 + i, 0, 0)),
            pl.BlockSpec((1, D_MODEL), lambda i, j: (0, 0)),
            pl.BlockSpec((D_MODEL, Z_W), lambda i, j: (0, 0)),
            pl.BlockSpec((1, Z_W), lambda i, j: (0, 0)),
        ],
        out_specs=pl.BlockSpec((bb * t, Z_W), lambda i, j: (i * nseq + j, 0)),
        out_shape=jax.ShapeDtypeStruct((nb * seq, Z_W), F32),
        compiler_params=pltpu.CompilerParams(
            dimension_semantics=("parallel", "parallel"), vmem_limit_bytes=VMEM_LIMIT_BYTES),
        name="inproj",
    )(x, mod, npre, w, b)


def _ffn_kernel(x_ref, ym_ref, mod_ref, wout_ref, w1_ref, w2_ref, npost_ref, npre_ref, npostf_ref,
                o_ref, x1_s, h2_s, acc_s):
    j = pl.program_id(2)
    bb, t, _ = x_ref.shape

    @pl.when(j == 0)
    def _():
        mod = mod_ref[...]
        g1 = mod[:, :, 2 * D_MODEL:3 * D_MODEL]
        sh2 = mod[:, :, 3 * D_MODEL:4 * D_MODEL]
        sc2 = mod[:, :, 4 * D_MODEL:5 * D_MODEL]
        y = jnp.dot(ym_ref[...].astype(BF16), wout_ref[...], preferred_element_type=F32)
        x1 = x_ref[...] + g1 * _rms(y.reshape(bb, t, D_MODEL), npost_ref[...])
        x1_s[...] = x1
        h2 = _rms(x1, npre_ref[...]) * (1.0 + sc2) + sh2
        h2_s[...] = h2.reshape(bb * t, D_MODEL).astype(BF16)
        acc_s[...] = jnp.zeros_like(acc_s)

    a = jnp.dot(h2_s[...], w1_ref[...], preferred_element_type=F32)
    a = jnp.square(jnp.maximum(a, 0.0))
    acc_s[...] += jnp.dot(a.astype(BF16), w2_ref[...], preferred_element_type=F32)

    @pl.when(j == pl.num_programs(2) - 1)
    def _():
        g2 = mod_ref[...][:, :, 5 * D_MODEL:6 * D_MODEL]
        f = acc_s[...].reshape(bb, t, D_MODEL)
        o_ref[...] = x1_s[...] + g2 * _rms(f, npostf_ref[...])


def _ffn_call(x, ymix, mod, layer, mod_row0, wout, w1, w2, npost, npre, npostf, bb, t):
    nb, seq, _ = x.shape
    nseq = seq // t
    nff = D_FF // FF_BLOCK
    mod_blk0 = mod_row0 // bb
    return pl.pallas_call(
        _ffn_kernel,
        grid=(nb // bb, nseq, nff),
        in_specs=[
            pl.BlockSpec((bb, t, D_MODEL), lambda i, s, j: (i, s, 0)),
            pl.BlockSpec((bb * t, D_MIX), lambda i, s, j: (i * nseq + s, 0)),
            pl.BlockSpec((None, bb, 1, 6 * D_MODEL), lambda i, s, j: (layer, mod_blk0 + i, 0, 0)),
            pl.BlockSpec((D_MIX, D_MODEL), lambda i, s, j: (0, 0)),
            pl.BlockSpec((D_MODEL, FF_BLOCK), lambda i, s, j: (0, j)),
            pl.BlockSpec((FF_BLOCK, D_MODEL), lambda i, s, j: (j, 0)),
            pl.BlockSpec((1, D_MODEL), lambda i, s, j: (0, 0)),
            pl.BlockSpec((1, D_MODEL), lambda i, s, j: (0, 0)),
            pl.BlockSpec((1, D_MODEL), lambda i, s, j: (0, 0)),
        ],
        out_specs=pl.BlockSpec((bb, t, D_MODEL), lambda i, s, j: (i, s, 0)),
        out_shape=jax.ShapeDtypeStruct(x.shape, F32),
        scratch_shapes=[
            pltpu.VMEM((bb, t, D_MODEL), F32),
            pltpu.VMEM((bb * t, D_MODEL), BF16),
            pltpu.VMEM((bb * t, D_MODEL), F32),
        ],
        compiler_params=pltpu.CompilerParams(
            dimension_semantics=("parallel", "parallel", "arbitrary"), vmem_limit_bytes=VMEM_LIMIT_BYTES),
        name="outffn",
    )(x, ymix, mod, wout, w1, w2, npost, npre, npostf)


HALF = LANES // 2
assert HALF == HEAD_DIM


def _lane_low(width):
    return lax.broadcasted_iota(jnp.int32, (1, 1, width), 2) < width // 2


def _expand(col_a, col_b, width):
    return jnp.where(_lane_low(width), col_a, col_b)


def _half_sum(x):
    low = _lane_low(x.shape[2])
    return (jnp.sum(jnp.where(low, x, 0.0), axis=2, keepdims=True),
            jnp.sum(jnp.where(low, 0.0, x), axis=2, keepdims=True))


def _half_max(x):
    low = _lane_low(x.shape[2])
    return (jnp.max(jnp.where(low, x, -jnp.inf), axis=2, keepdims=True),
            jnp.max(jnp.where(low, -jnp.inf, x), axis=2, keepdims=True))


def _block_diag(x):
    low = _lane_low(x.shape[2])
    return jnp.concatenate([jnp.where(low, x, 0.0), jnp.where(low, 0.0, x)], axis=1)


def _pair_rmsnorm_gate(o, wn, gate):
    sa, sb = _half_sum(o * o)
    ms = _expand(sa, sb, LANES) * (1.0 / HEAD_DIM)
    return o * lax.rsqrt(ms + EPS) * wn * gate


def _mixer_kernel(layer, c, sb, grp,
                  z_ref, prm_ref, cw_ref, lbl_ref, wn_ref, c0_ref, n0_ref, m0_ref, s0_ref, cv0_ref, h0_ref,
                  y_ref, c1_ref, n1_ref, m1_ref, s1_ref, cv1_ref, h1_ref,
                  xbuf, c_bd, n_pr, m_pr, s_bd, h_bd):
    step = pl.program_id(1)
    last = pl.num_programs(1) - 1
    first = SUBLANES - (CONV_W - 1)
    mp, gp, hp = M_HEADS // 2, G_HEADS // 2, H_HEADS // 2
    c2 = 2 * c

    def load_bd(dst, src, npairs):
        dst[...] = jnp.zeros_like(dst)
        for p in range(npairs):
            dst[p * grp:(p + 1) * grp, 0:HALF, 0:HALF] = src[:, 2 * p]
            dst[p * grp:(p + 1) * grp, HALF:LANES, HALF:LANES] = src[:, 2 * p + 1]

    def store_bd(dst, src, npairs):
        for p in range(npairs):
            dst[:, 2 * p] = src[p * grp:(p + 1) * grp, 0:HALF, 0:HALF]
            dst[:, 2 * p + 1] = src[p * grp:(p + 1) * grp, HALF:LANES, HALF:LANES]

    @pl.when(step == 0)
    def _():
        load_bd(c_bd, c0_ref, mp)
        load_bd(s_bd, s0_ref, gp)
        load_bd(h_bd, h0_ref, hp)
        for p in range(mp):
            n_pr[p * grp:(p + 1) * grp] = jnp.concatenate(
                [n0_ref[:, 2 * p:2 * p + 1, :], n0_ref[:, 2 * p + 1:2 * p + 2, :]], axis=2)
            m_pr[p * grp:(p + 1) * grp] = _expand(m0_ref[:, :, 2 * p:2 * p + 1], m0_ref[:, :, 2 * p + 1:2 * p + 2],
                                                  LANES)
        xbuf[:, first:SUBLANES, :] = cv0_ref[...]

    row = lax.broadcasted_iota(jnp.int32, (c, c2), 0)
    col = lax.broadcasted_iota(jnp.int32, (c, c2), 1) % c
    incl = row >= col
    strict = row > col
    eye_c = (row == col).astype(F32)
    rc = lax.broadcasted_iota(jnp.int32, (c, c), 0)
    cc = lax.broadcasted_iota(jnp.int32, (c, c), 1)
    tri = (rc >= cc).astype(F32)
    r128 = lax.broadcasted_iota(jnp.int32, (LANES, LANES), 0)
    c128 = lax.broadcasted_iota(jnp.int32, (LANES, LANES), 1)
    same_head = (r128 < HALF) == (c128 < HALF)
    ones_bd = same_head.astype(F32)
    eye_128 = (r128 == c128).astype(F32)

    def pairs_of(ref_or_val, off, npairs):
        return jnp.concatenate(
            [ref_or_val[:, :, off + p * LANES:off + (p + 1) * LANES] for p in range(npairs)], axis=0)

    def col_pairs(a, off, npairs, width):
        return jnp.concatenate(
            [_expand(a[:, :, off + 2 * p:off + 2 * p + 1], a[:, :, off + 2 * p + 1:off + 2 * p + 2], width)
             for p in range(npairs)], axis=0)

    def row_pairs(a_t, off, npairs):
        return jnp.concatenate(
            [jnp.concatenate([a_t[:, off + 2 * p:off + 2 * p + 1, :], a_t[:, off + 2 * p + 1:off + 2 * p + 2, :]],
                             axis=2) for p in range(npairs)], axis=0)

    def norm_weights(off, npairs):
        return jnp.concatenate(
            [jnp.broadcast_to(wn_ref[:, off + p * LANES:off + (p + 1) * LANES][None], (grp, 1, LANES))
             for p in range(npairs)], axis=0)

    def store_pairs(val, off, npairs):
        for p in range(npairs):
            y_ref[:, :, off + p * LANES:off + (p + 1) * LANES] = val[p * grp:(p + 1) * grp]

    zg = z_ref[:, :, OFF_GATE:OFF_GATE + LANES]
    lane = lax.broadcasted_iota(jnp.int32, (1, 1, LANES), 2)
    tg = zg + prm_ref[0:1, :]
    tail = _log1pexp_negabs(tg)
    logsig = -(jnp.maximum(-tg, 0.0) + tail)
    splus = jnp.maximum(tg, 0.0) + tail
    lgv = -jnp.exp(prm_ref[1:2, :]) * splus
    gates = jnp.where(lane < GATE_MF, zg,
                      jnp.where(lane < GATE_GB, logsig,
                                jnp.where(lane < GATE_GA, _sigmoid(tg),
                                          jnp.where(lane < GATE_END, lgv, 0.0))))
    cs = jnp.stack([_mm_f32(tri, gates[g]) for g in range(grp)])
    gates_t = jnp.stack([gates[g].T for g in range(grp)])
    cs_t = jnp.stack([cs[g].T for g in range(grp)])

    q = pairs_of(z_ref, OFF_MQ, mp)
    k = pairs_of(z_ref, OFF_MK, mp) * QK_SCALE
    v = pairs_of(z_ref, OFF_MV, mp)
    ig_row = row_pairs(gates_t, GATE_MI, mp)
    b_row = row_pairs(cs_t, GATE_MF, mp)
    b_col_s = col_pairs(cs, GATE_MF, mp, c2)
    b_col_d = col_pairs(cs, GATE_MF, mp, LANES)
    ig_col_d = col_pairs(gates, GATE_MI, mp, LANES)
    m_prev_d = m_pr[...]
    m_prev_s = _expand(m_prev_d[:, :, 0:1], m_prev_d[:, :, HALF:HALF + 1], c2)
    dmat = jnp.where(incl, b_col_s - b_row + ig_row, -jnp.inf)
    rmax_a, rmax_b = _half_max(dmat)
    inter_s = b_col_s + m_prev_s
    inter_d = b_col_d + m_prev_d
    m_t_s = jnp.maximum(inter_s, _expand(rmax_a, rmax_b, c2))
    m_t_d = jnp.maximum(inter_d, _expand(rmax_a, rmax_b, LANES))
    w = jnp.exp(dmat - m_t_s)
    wi_d = jnp.exp(inter_d - m_t_d)
    wqk = w * _bmm_nt(q, _block_diag(k))
    c_st = c_bd[...]
    n_st = n_pr[...]
    num = _bmm(wqk, _block_diag(v)) + wi_d * _bmm(q, c_st)
    ws_a, ws_b = _half_sum(wqk)
    qn_a, qn_b = _half_sum(q * n_st)
    den = _expand(ws_a, ws_b, LANES) + wi_d * _expand(qn_a, qn_b, LANES)
    hm = num / jnp.maximum(jnp.abs(den), 1.0)
    m_new_d = m_t_d[:, c - 1:c, :]
    b_last_d = b_col_d[:, c - 1:c, :]
    ws = jnp.exp(b_last_d - b_col_d + ig_col_d - m_new_d)
    decay_d = jnp.exp(b_last_d + m_prev_d - m_new_d)
    kw = k * ws
    c_bd[...] = decay_d * c_st + jnp.where(same_head, _bmm_tn(kw, v), 0.0)
    n_pr[...] = decay_d * n_st + jnp.sum(kw, axis=1, keepdims=True)
    m_pr[...] = m_new_d
    store_pairs(_pair_rmsnorm_gate(hm, norm_weights(0, mp), _sigmoid(pairs_of(z_ref, OFF_MO, mp))), 0, mp)

    xbuf[:, SUBLANES:SUBLANES + c, :] = z_ref[:, :, OFF_GQKV:OFF_GQKV + 3 * G_W]
    pre = cw_ref[0:1, :] * xbuf[:, first:first + c, :]
    for j in range(1, CONV_W):
        pre = pre + cw_ref[j:j + 1, :] * xbuf[:, first + j:first + j + c, :]
    conv = pre * _sigmoid(pre)
    cv1_ref[...] = xbuf[:, c + first:c + SUBLANES, :]
    xbuf[:, 0:SUBLANES, :] = xbuf[:, c:c + SUBLANES, :]

    gq = pairs_of(conv, 0, gp)
    gk = pairs_of(conv, G_W, gp)
    gv = pairs_of(conv, 2 * G_W, gp)
    qq_a, qq_b = _half_sum(gq * gq)
    kk_a, kk_b = _half_sum(gk * gk)
    gq = gq * lax.rsqrt(_expand(qq_a, qq_b, LANES) + EPS) * QK_SCALE
    gk = gk * lax.rsqrt(_expand(kk_a, kk_b, LANES) + EPS)
    beta_d = col_pairs(gates, GATE_GB, gp, LANES)
    g_col_s = col_pairs(cs, GATE_GA, gp, c2)
    g_col_d = col_pairs(cs, GATE_GA, gp, LANES)
    g_row = row_pairs(cs_t, GATE_GA, gp)
    ediff = jnp.exp(jnp.where(incl, g_col_s - g_row, 0.0))
    kb = gk * beta_d
    gk_bd = _block_diag(gk)
    lm = jnp.where(strict, _bmm_nt(kb, gk_bd) * ediff, 0.0)
    tinv = None
    s = 1
    while s < c:
        same = (row // (2 * s)) == (col // (2 * s))
        off = same & ((row % (2 * s)) >= s) & ((col % (2 * s)) < s)
        coff = jnp.where(off, lm, 0.0)
        if tinv is None:
            tinv = eye_c - coff
        else:
            tinv = tinv - _bmm(_bmm(tinv, _block_diag(coff)), _block_diag(tinv))
        s *= 2
    eg_d = jnp.exp(g_col_d)
    u = _bmm(tinv, _block_diag(gv * beta_d))
    wk = _bmm(tinv, _block_diag(kb * eg_d))
    s_st = s_bd[...]
    v_new = u - _bmm(wk, s_st)
    attn = jnp.where(incl, _bmm_nt(gq, gk_bd) * ediff, 0.0)
    o = _bmm(gq * eg_d, s_st) + _bmm(attn, _block_diag(v_new))
    gl_d = g_col_d[:, c - 1:c, :]
    s_bd[...] = s_st * jnp.exp(gl_d) + jnp.where(same_head, _bmm_tn(gk * jnp.exp(gl_d - g_col_d), v_new), 0.0)
    gg = pairs_of(z_ref, OFF_GG, gp)
    store_pairs(_pair_rmsnorm_gate(o, norm_weights(M_W, gp), gg * _sigmoid(gg)), M_W, gp)

    hf = z_ref[:, :, OFF_HF:OFF_HF + H_W]
    ls = -(jnp.maximum(-hf, 0.0) + _log1pexp_negabs(hf))
    if layer == 0:
        lf = ls
        kk = _sigmoid(-hf)
    else:
        lg = lbl_ref[...]
        ex = jnp.exp(lg - jnp.max(lg, axis=0, keepdims=True))
        sm = ex / jnp.sum(ex, axis=0, keepdims=True)
        cum = sm[0:1, :]
        for j in range(1, layer + 1):
            cum = cum + sm[j:j + 1, :]
        lb = cum - sm[0:1, :]
        a = jnp.log(lb)
        b = jnp.log1p(-lb) + ls
        lf = jnp.maximum(a, b) + _log1pexp_negabs(a - b)
        kk = (1.0 - lb) * _sigmoid(-hf)
    gcum = jnp.stack([_mm_f32(tri, lf[g]) for g in range(grp)])
    nh = hp * grp
    trow = lax.broadcasted_iota(jnp.int32, (sb, LANES), 0)
    q = pairs_of(z_ref, OFF_HQ, hp)
    k = pairs_of(kk, 0, hp)
    v = pairs_of(z_ref, OFF_HI, hp)
    gh = pairs_of(gcum, 0, hp)
    s_st = h_bd[...]
    inter = _bmm(q * jnp.exp(gh), s_st)
    blocks = []
    for blk in range(c // sb):
        r0 = blk * sb
        qi = q[:, r0:r0 + sb]
        ki = k[:, r0:r0 + sb]
        vi = v[:, r0:r0 + sb]
        gi = gh[:, r0:r0 + sb]
        prods = []
        for s in range(sb):
            e = jnp.exp(jnp.where(trow >= s, gi - gi[:, s:s + 1], -jnp.inf))
            prods.append(qi * ki[:, s:s + 1] * e)
        stacked = jnp.concatenate(prods, axis=1).reshape(nh * sb * sb, LANES)
        rs = _mm(stacked, ones_bd).reshape(nh, sb * sb, LANES)
        oi = rs[:, 0:sb] * vi[:, 0:1]
        for s in range(1, sb):
            oi = oi + rs[:, s * sb:(s + 1) * sb] * vi[:, s:s + 1]
        if blk > 0:
            ref = gh[:, r0:r0 + 1]
            qt = qi * jnp.exp(gi - ref)
            kt = k[:, 0:r0] * jnp.exp(ref - gh[:, 0:r0])
            oi = oi + _bmm(_bmm_nt(qt, _block_diag(kt)), _block_diag(v[:, 0:r0]))
        blocks.append(oi)
    o = inter + (blocks[0] if len(blocks) == 1 else jnp.concatenate(blocks, axis=1))
    gl = gh[:, c - 1:c]
    decay_col = jnp.sum(eye_128 * jnp.exp(gl), axis=2, keepdims=True)
    h_bd[...] = decay_col * s_st + jnp.where(same_head, _bmm_tn(k * jnp.exp(gl - gh), v), 0.0)
    hg = pairs_of(z_ref, OFF_HG, hp)
    store_pairs(_pair_rmsnorm_gate(o, norm_weights(M_W + G_W, hp), hg * _sigmoid(hg)), M_W + G_W, hp)

    @pl.when(step == last)
    def _():
        store_bd(c1_ref, c_bd, mp)
        store_bd(s1_ref, s_bd, gp)
        store_bd(h1_ref, h_bd, hp)
        for p in range(mp):
            n1_ref[:, 2 * p:2 * p + 1, :] = n_pr[p * grp:(p + 1) * grp, :, 0:HALF]
            n1_ref[:, 2 * p + 1:2 * p + 2, :] = n_pr[p * grp:(p + 1) * grp, :, HALF:LANES]
            m1_ref[:, :, 2 * p:2 * p + 1] = m_pr[p * grp:(p + 1) * grp, :, 0:1]
            m1_ref[:, :, 2 * p + 1:2 * p + 2] = m_pr[p * grp:(p + 1) * grp, :, HALF:HALF + 1]


def _mixer_call(layer, z, prm, cw, lbl, wn, init, init_layer, c, sb, grp):
    nb, seq, _ = z.shape
    nchunks = seq // c
    kern = functools.partial(_mixer_kernel, layer, c, sb, grp)
    const2 = lambda b, n: (0, 0)
    st5 = lambda b, n: (init_layer, b, 0, 0, 0)
    st4 = lambda b, n: (init_layer, b, 0, 0)
    o4 = lambda b, n: (b, 0, 0, 0)
    o3 = lambda b, n: (b, 0, 0)
    state_blocks = ((grp, M_HEADS, HEAD_DIM, HEAD_DIM), (grp, M_HEADS, HEAD_DIM), (grp, 1, M_HEADS),
                    (grp, G_HEADS, HEAD_DIM, HEAD_DIM), (grp, CONV_W - 1, 3 * G_W),
                    (grp, H_HEADS, HEAD_DIM, HEAD_DIM))
    return pl.pallas_call(
        kern,
        grid=(nb // grp, nchunks),
        in_specs=[
            pl.BlockSpec((grp, c, Z_W), lambda b, n: (b, n, 0)),
            pl.BlockSpec(prm.shape, const2),
            pl.BlockSpec(cw.shape, const2),
            pl.BlockSpec(lbl.shape, const2),
            pl.BlockSpec(wn.shape, const2),
        ] + [pl.BlockSpec((None,) + blk, st5 if len(blk) == 4 else st4) for blk in state_blocks],
        out_specs=[pl.BlockSpec((grp, c, D_MIX), lambda b, n: (b, n, 0))]
                  + [pl.BlockSpec(blk, o4 if len(blk) == 4 else o3) for blk in state_blocks],
        out_shape=[jax.ShapeDtypeStruct((nb, seq, D_MIX), F32)]
                  + [jax.ShapeDtypeStruct((nb,) + blk[1:], F32) for blk in state_blocks],
        scratch_shapes=[
            pltpu.VMEM((grp, SUBLANES + c, 3 * G_W), F32),
            pltpu.VMEM((M_HEADS // 2 * grp, LANES, LANES), F32),
            pltpu.VMEM((M_HEADS // 2 * grp, 1, LANES), F32),
            pltpu.VMEM((M_HEADS // 2 * grp, 1, LANES), F32),
            pltpu.VMEM((G_HEADS // 2 * grp, LANES, LANES), F32),
            pltpu.VMEM((H_HEADS // 2 * grp, LANES, LANES), F32),
        ],
        compiler_params=pltpu.CompilerParams(
            dimension_semantics=("parallel", "arbitrary"), vmem_limit_bytes=VMEM_LIMIT_BYTES),
        name="mixer",
    )(z, prm, cw, lbl, wn, *init)


def _permute_in_proj(w_in, b_in):
    o = 0
    pieces = {}
    for name, width in (("mq", M_W), ("mk", M_W), ("mv", M_W), ("mo", M_W), ("mi", M_HEADS), ("mf", M_HEADS),
                        ("gqkv", 3 * G_W), ("gg", G_W), ("gb", G_HEADS), ("ga", G_HEADS),
                        ("hq", H_W), ("hf", H_W), ("hi", H_W), ("hg", H_W)):
        pieces[name] = (o, o + width)
        o += width
    order = ("mq", "mk", "mv", "mo", "gqkv", "gg", "hq", "hf", "hi", "hg", "mi", "mf", "gb", "ga")

    def perm(a):
        cols = [a[..., pieces[n][0]:pieces[n][1]] for n in order]
        pad = jnp.zeros(a.shape[:-1] + (LANES - GATE_END,), a.dtype)
        return jnp.concatenate(cols + [pad], axis=-1)

    return perm(w_in).astype(BF16), perm(b_in).reshape(DEPTH, 1, Z_W)


def _gate_params(mlstm_f_bias, gdn_A_log, gdn_dt_bias):
    prm = jnp.zeros((DEPTH, SUBLANES, LANES), F32)
    prm = prm.at[:, 0, GATE_MF:GATE_MF + M_HEADS].set(mlstm_f_bias.astype(F32))
    prm = prm.at[:, 0, GATE_GA:GATE_GA + G_HEADS].set(gdn_dt_bias.astype(F32))
    prm = prm.at[:, 1, GATE_GA:GATE_GA + G_HEADS].set(gdn_A_log.astype(F32))
    return prm


def _trunk(x, mod, mod_row0, init, init_is_shared, params, c, sb, grp, in_bt, ffn_bt):
    (npre_mix, npost_mix, w_in, b_in, prm, conv_w, lb_logits, out_norm, w_out, npre_ffn, npost_ffn, w_ff1,
     w_ff2) = params
    nb, seq, _ = x.shape
    new = ([], [], [], [], [], [])
    for l in range(DEPTH):
        z = _inproj_call(x, mod, l, mod_row0, npre_mix[l], w_in[l], b_in[l], *in_bt)
        outs = _mixer_call(l, z.reshape(nb, seq, Z_W), prm[l], conv_w[l], lb_logits, out_norm[l],
                           init, 0 if init_is_shared else l, c, sb, grp)
        ymix = outs[0].reshape(nb * seq, D_MIX)
        x = _ffn_call(x, ymix, mod, l, mod_row0, w_out[l], w_ff1[l], w_ff2[l], npost_mix[l], npre_ffn[l],
                      npost_ffn[l], *ffn_bt)
        for lst, s in zip(new, outs[1:]):
            lst.append(s)
    c1, n1, m1, s1, cv1, h1 = (jnp.stack(s) for s in new)
    return x, (c1, n1, m1.reshape(DEPTH, nb, M_HEADS), s1, cv1, h1)


def kernel(x_prompt, x_sample, state_mlstm_C, state_mlstm_n, state_mlstm_m, state_gdn_S, state_gdn_conv,
           state_hgrn_S, c_prompt, c_sample, w_ada, b_ada, norm_pre_mix, norm_post_mix, w_in, b_in,
           mlstm_f_bias, gdn_conv_w, gdn_A_log, gdn_dt_bias, hgrn_lb_logits, mix_out_norm, w_out,
           norm_pre_ffn, norm_post_ffn, w_ff1, w_ff2):
    bp, seq_p, _ = x_prompt.shape
    bs, seq_s, _ = x_sample.shape

    mod = _ada_call(jnp.concatenate([c_sample, c_prompt], axis=0), w_ada, b_ada)

    w_in_p, b_in_p = _permute_in_proj(w_in, b_in)
    row = lambda a: a.reshape(DEPTH, 1, a.shape[-1]).astype(F32)
    params = (row(norm_pre_mix), row(norm_post_mix), w_in_p, b_in_p,
              _gate_params(mlstm_f_bias, gdn_A_log, gdn_dt_bias), gdn_conv_w.astype(F32),
              hgrn_lb_logits.astype(F32), row(mix_out_norm), w_out.astype(BF16), row(norm_pre_ffn),
              row(norm_post_ffn), w_ff1.astype(BF16), w_ff2.astype(BF16))

    zeros = (jnp.zeros((1, bp, M_HEADS, HEAD_DIM, HEAD_DIM), F32),
             jnp.zeros((1, bp, M_HEADS, HEAD_DIM), F32),
             jnp.zeros((1, bp, 1, M_HEADS), F32),
             jnp.zeros((1, bp, G_HEADS, HEAD_DIM, HEAD_DIM), F32),
             jnp.zeros((1, bp, CONV_W - 1, 3 * G_W), F32),
             jnp.zeros((1, bp, H_HEADS, HEAD_DIM, HEAD_DIM), F32))
    past = (state_mlstm_C.astype(F32), state_mlstm_n.astype(F32),
            state_mlstm_m.astype(F32).reshape(DEPTH, bs, 1, M_HEADS), state_gdn_S.astype(F32),
            state_gdn_conv.astype(F32), state_hgrn_S.astype(F32))

    y_p, st_p = _trunk(x_prompt, mod, bs, zeros, True, params, PROMPT_CHUNK, HGRN_SUBBLOCK, PROMPT_GROUP,
                       (1, IN_TILE), (1, FFN_TILE))
    y_s, st_s = _trunk(x_sample, mod, 0, past, False, params, seq_s, min(seq_s, HGRN_SUBBLOCK), SAMPLE_GROUP,
                       (IN_TILE // seq_s, seq_s), (FFN_TILE // seq_s, seq_s))
    return (y_p, y_s) + st_p + st_s
```

```python
import functools

import jax
import jax.numpy as jnp
from jax import lax
from jax.experimental import pallas as pl
from jax.experimental.pallas import tpu as pltpu

F32 = jnp.float32
BF16 = jnp.bfloat16

D_MODEL = 1024
DEPTH = 4
HEAD_DIM = 64
M_HEADS = 6
G_HEADS = 6
H_HEADS = 4
M_W = M_HEADS * HEAD_DIM
G_W = G_HEADS * HEAD_DIM
H_W = H_HEADS * HEAD_DIM
D_MIX = M_W + G_W + H_W
CONV_W = 4
D_FF = 4 * D_MODEL
EPS = 1e-6
QK_SCALE = HEAD_DIM ** -0.5

LANES = 128
SUBLANES = 8
VMEM_LIMIT_BYTES = 56 * 1024 * 1024

OFF_MQ = 0
OFF_MK = OFF_MQ + M_W
OFF_MV = OFF_MK + M_W
OFF_MO = OFF_MV + M_W
OFF_GQKV = OFF_MO + M_W
OFF_GG = OFF_GQKV + 3 * G_W
OFF_HQ = OFF_GG + G_W
OFF_HF = OFF_HQ + H_W
OFF_HI = OFF_HF + H_W
OFF_HG = OFF_HI + H_W
OFF_GATE = OFF_HG + H_W
Z_W = OFF_GATE + LANES
GATE_MI = 0
GATE_MF = GATE_MI + M_HEADS
GATE_GB = GATE_MF + M_HEADS
GATE_GA = GATE_GB + G_HEADS
GATE_END = GATE_GA + G_HEADS

PROMPT_CHUNK = 64
HGRN_SUBBLOCK = 8
PROMPT_GROUP = 8
SAMPLE_GROUP = 16
IN_TILE = 512
FFN_TILE = 512
FF_BLOCK = 2048
ADA_BLOCK = 1536


def _mm(a, b):
    return jnp.dot(a.astype(BF16), b.astype(BF16), preferred_element_type=F32)


def _mm_f32(a, b):
    return jnp.dot(a, b, preferred_element_type=F32, precision=lax.Precision.HIGHEST)


def _bmm(a, b):
    return jnp.einsum("nmk,nkp->nmp", a.astype(BF16), b.astype(BF16), preferred_element_type=F32)


def _bmm_nt(a, b):
    return jnp.einsum("nmk,npk->nmp", a.astype(BF16), b.astype(BF16), preferred_element_type=F32)


def _bmm_tn(a, b):
    return jnp.einsum("nsk,nsp->nkp", a.astype(BF16), b.astype(BF16), preferred_element_type=F32)


def _sigmoid(x):
    return 0.5 * jnp.tanh(0.5 * x) + 0.5


def _log1pexp_negabs(x):
    return jnp.log1p(jnp.exp(-jnp.abs(x)))


def _rms(x, w):
    ms = jnp.mean(x * x, axis=-1, keepdims=True)
    return x * lax.rsqrt(ms + EPS) * w


def _ada_kernel(c_ref, w_ref, b_ref, o_ref):
    mod = jnp.dot(c_ref[...].astype(BF16), w_ref[...].astype(BF16), preferred_element_type=F32) + b_ref[...]
    o_ref[...] = mod[:, None, :]


def _ada_call(c_all, w_ada, b_ada):
    rows = c_all.shape[0]
    nblk = (6 * D_MODEL) // ADA_BLOCK
    return pl.pallas_call(
        _ada_kernel,
        grid=(DEPTH, nblk),
        in_specs=[
            pl.BlockSpec((rows, D_MODEL), lambda l, j: (0, 0)),
            pl.BlockSpec((None, D_MODEL, ADA_BLOCK), lambda l, j: (l, 0, j)),
            pl.BlockSpec((None, 1, ADA_BLOCK), lambda l, j: (l, 0, j)),
        ],
        out_specs=pl.BlockSpec((None, rows, 1, ADA_BLOCK), lambda l, j: (l, 0, 0, j)),
        out_shape=jax.ShapeDtypeStruct((DEPTH, rows, 1, 6 * D_MODEL), F32),
        compiler_params=pltpu.CompilerParams(
            dimension_semantics=("parallel", "parallel"), vmem_limit_bytes=VMEM_LIMIT_BYTES),
        name="adaln",
    )(c_all, w_ada, b_ada.reshape(DEPTH, 1, 6 * D_MODEL))


def _inproj_kernel(x_ref, mod_ref, npre_ref, w_ref, b_ref, z_ref):
    bb, t, _ = x_ref.shape
    mod = mod_ref[...]
    sh1 = mod[:, :, 0:D_MODEL]
    sc1 = mod[:, :, D_MODEL:2 * D_MODEL]
    h = _rms(x_ref[...], npre_ref[...]) * (1.0 + sc1) + sh1
    hb = h.reshape(bb * t, D_MODEL).astype(BF16)
    z_ref[...] = jnp.dot(hb, w_ref[...], preferred_element_type=F32) + b_ref[...]


def _inproj_call(x, mod, layer, mod_row0, npre, w, b, bb, t):
    nb, seq, _ = x.shape
    grid = (nb // bb, seq // t)
    nseq = seq // t
    mod_blk0 = mod_row0 // bb
    return pl.pallas_call(
        _inproj_kernel,
        grid=grid,
        in_specs=[
            pl.BlockSpec((bb, t, D_MODEL), lambda i, j: (i, j, 0)),
            pl.BlockSpec((None, bb, 1, 6 * D_MODEL), lambda i, j: (layer, mod_blk0 + i, 0, 0)),
            pl.BlockSpec((1, D_MODEL), lambda i, j: (0, 0)),
            pl.BlockSpec((D_MODEL, Z_W), lambda i, j: (0, 0)),
            pl.BlockSpec((1, Z_W), lambda i, j: (0, 0)),
        ],
        out_specs=pl.BlockSpec((bb * t, Z_W), lambda i, j: (i * nseq + j, 0)),
        out_shape=jax.ShapeDtypeStruct((nb * seq, Z_W), F32),
        compiler_params=pltpu.CompilerParams(
            dimension_semantics=("parallel", "parallel"), vmem_limit_bytes=VMEM_LIMIT_BYTES),
        name="inproj",
    )(x, mod, npre, w, b)


def _ffn_kernel(x_ref, ym_ref, mod_ref, wout_ref, w1_ref, w2_ref, npost_ref, npre_ref, npostf_ref,
                o_ref, x1_s, h2_s, acc_s):
    j = pl.program_id(2)
    bb, t, _ = x_ref.shape

    @pl.when(j == 0)
    def _():
        mod = mod_ref[...]
        g1 = mod[:, :, 2 * D_MODEL:3 * D_MODEL]
        sh2 = mod[:, :, 3 * D_MODEL:4 * D_MODEL]
        sc2 = mod[:, :, 4 * D_MODEL:5 * D_MODEL]
        y = jnp.dot(ym_ref[...].astype(BF16), wout_ref[...], preferred_element_type=F32)
        x1 = x_ref[...] + g1 * _rms(y.reshape(bb, t, D_MODEL), npost_ref[...])
        x1_s[...] = x1
        h2 = _rms(x1, npre_ref[...]) * (1.0 + sc2) + sh2
        h2_s[...] = h2.reshape(bb * t, D_MODEL).astype(BF16)
        acc_s[...] = jnp.zeros_like(acc_s)

    a = jnp.dot(h2_s[...], w1_ref[...], preferred_element_type=F32)
    a = jnp.square(jnp.maximum(a, 0.0))
    acc_s[...] += jnp.dot(a.astype(BF16), w2_ref[...], preferred_element_type=F32)

    @pl.when(j == pl.num_programs(2) - 1)
    def _():
        g2 = mod_ref[...][:, :, 5 * D_MODEL:6 * D_MODEL]
        f = acc_s[...].reshape(bb, t, D_MODEL)
        o_ref[...] = x1_s[...] + g2 * _rms(f, npostf_ref[...])


def _ffn_call(x, ymix, mod, layer, mod_row0, wout, w1, w2, npost, npre, npostf, bb, t):
    nb, seq, _ = x.shape
    nseq = seq // t
    nff = D_FF // FF_BLOCK
    mod_blk0 = mod_row0 // bb
    return pl.pallas_call(
        _ffn_kernel,
        grid=(nb // bb, nseq, nff),
        in_specs=[
            pl.BlockSpec((bb, t, D_MODEL), lambda i, s, j: (i, s, 0)),
            pl.BlockSpec((bb * t, D_MIX), lambda i, s, j: (i * nseq + s, 0)),
            pl.BlockSpec((None, bb, 1, 6 * D_MODEL), lambda i, s, j: (layer, mod_blk0 + i, 0, 0)),
            pl.BlockSpec((D_MIX, D_MODEL), lambda i, s, j: (0, 0)),
            pl.BlockSpec((D_MODEL, FF_BLOCK), lambda i, s, j: (0, j)),
            pl.BlockSpec((FF_BLOCK, D_MODEL), lambda i, s, j: (j, 0)),
            pl.BlockSpec((1, D_MODEL), lambda i, s, j: (0, 0)),
            pl.BlockSpec((1, D_MODEL), lambda i, s, j: (0, 0)),
            pl.BlockSpec((1, D_MODEL), lambda i, s, j: (0, 0)),
        ],
        out_specs=pl.BlockSpec((bb, t, D_MODEL), lambda i, s, j: (i, s, 0)),
        out_shape=jax.ShapeDtypeStruct(x.shape, F32),
        scratch_shapes=[
            pltpu.VMEM((bb, t, D_MODEL), F32),
            pltpu.VMEM((bb * t, D_MODEL), BF16),
            pltpu.VMEM((bb * t, D_MODEL), F32),
        ],
        compiler_params=pltpu.CompilerParams(
            dimension_semantics=("parallel", "parallel", "arbitrary"), vmem_limit_bytes=VMEM_LIMIT_BYTES),
        name="outffn",
    )(x, ymix, mod, wout, w1, w2, npost, npre, npostf)


HALF = LANES // 2
assert HALF == HEAD_DIM


def _lane_low(width):
    return lax.broadcasted_iota(jnp.int32, (1, 1, width), 2) < width // 2


def _expand(col_a, col_b, width):
    return jnp.where(_lane_low(width), col_a, col_b)


def _half_sum(x):
    low = _lane_low(x.shape[2])
    return (jnp.sum(jnp.where(low, x, 0.0), axis=2, keepdims=True),
            jnp.sum(jnp.where(low, 0.0, x), axis=2, keepdims=True))


def _half_max(x):
    low = _lane_low(x.shape[2])
    return (jnp.max(jnp.where(low, x, -jnp.inf), axis=2, keepdims=True),
            jnp.max(jnp.where(low, -jnp.inf, x), axis=2, keepdims=True))


def _block_diag(x):
    low = _lane_low(x.shape[2])
    bf16_rows = 2 * SUBLANES
    xb = x.astype(BF16) if x.shape[1] % bf16_rows == 0 else x
    zero = jnp.zeros_like(xb)
    return jnp.concatenate([jnp.where(low, xb, zero), jnp.where(low, zero, xb)], axis=1).astype(BF16)


def _pair_rmsnorm_gate(o, wn, gate):
    sa, sb = _half_sum(o * o)
    ms = _expand(sa, sb, LANES) * (1.0 / HEAD_DIM)
    return o * lax.rsqrt(ms + EPS) * wn * gate


def _mixer_kernel(layer, c, sb, grp,
                  z_ref, prm_ref, cw_ref, lbl_ref, wn_ref, c0_ref, n0_ref, m0_ref, s0_ref, cv0_ref, h0_ref,
                  acc_c, acc_n, acc_m, acc_s, acc_cv, acc_h,
                  y_ref, c1_ref, n1_ref, m1_ref, s1_ref, cv1_ref, h1_ref,
                  xbuf, c_bd, n_pr, m_pr, s_bd, h_bd):
    del acc_c, acc_n, acc_m, acc_s, acc_cv, acc_h
    step = pl.program_id(1)
    last = pl.num_programs(1) - 1
    first = SUBLANES - (CONV_W - 1)
    mp, gp, hp = M_HEADS // 2, G_HEADS // 2, H_HEADS // 2
    c2 = 2 * c

    def load_bd(dst, src, npairs):
        dst[...] = jnp.zeros_like(dst)
        for p in range(npairs):
            dst[p * grp:(p + 1) * grp, 0:HALF, 0:HALF] = src[:, 2 * p]
            dst[p * grp:(p + 1) * grp, HALF:LANES, HALF:LANES] = src[:, 2 * p + 1]

    def store_bd(dst, src, npairs):
        for p in range(npairs):
            dst[:, 2 * p] = src[p * grp:(p + 1) * grp, 0:HALF, 0:HALF]
            dst[:, 2 * p + 1] = src[p * grp:(p + 1) * grp, HALF:LANES, HALF:LANES]

    @pl.when(step == 0)
    def _():
        load_bd(c_bd, c0_ref, mp)
        load_bd(s_bd, s0_ref, gp)
        load_bd(h_bd, h0_ref, hp)
        for p in range(mp):
            n_pr[p * grp:(p + 1) * grp] = jnp.concatenate(
                [n0_ref[:, 2 * p:2 * p + 1, :], n0_ref[:, 2 * p + 1:2 * p + 2, :]], axis=2)
            m_pr[p * grp:(p + 1) * grp] = _expand(m0_ref[:, :, 2 * p:2 * p + 1], m0_ref[:, :, 2 * p + 1:2 * p + 2],
                                                  LANES)
        xbuf[:, first:SUBLANES, :] = cv0_ref[...]

    row = lax.broadcasted_iota(jnp.int32, (c, c2), 0)
    col = lax.broadcasted_iota(jnp.int32, (c, c2), 1) % c
    incl = row >= col
    strict = row > col
    eye_c = (row == col).astype(F32)
    rc = lax.broadcasted_iota(jnp.int32, (c, c), 0)
    cc = lax.broadcasted_iota(jnp.int32, (c, c), 1)
    tri = (rc >= cc).astype(F32)
    r128 = lax.broadcasted_iota(jnp.int32, (LANES, LANES), 0)
    c128 = lax.broadcasted_iota(jnp.int32, (LANES, LANES), 1)
    same_head = (r128 < HALF) == (c128 < HALF)
    ones_bd = same_head.astype(F32)
    eye_128 = (r128 == c128).astype(F32)

    def pairs_of(ref_or_val, off, npairs):
        return jnp.concatenate(
            [ref_or_val[:, :, off + p * LANES:off + (p + 1) * LANES] for p in range(npairs)], axis=0)

    def col_pairs(a, off, npairs, width):
        return jnp.concatenate(
            [_expand(a[:, :, off + 2 * p:off + 2 * p + 1], a[:, :, off + 2 * p + 1:off + 2 * p + 2], width)
             for p in range(npairs)], axis=0)

    def row_pairs(a_t, off, npairs):
        return jnp.concatenate(
            [jnp.concatenate([a_t[:, off + 2 * p:off + 2 * p + 1, :], a_t[:, off + 2 * p + 1:off + 2 * p + 2, :]],
                             axis=2) for p in range(npairs)], axis=0)

    def norm_weights(off, npairs):
        return jnp.concatenate(
            [jnp.broadcast_to(wn_ref[:, off + p * LANES:off + (p + 1) * LANES][None], (grp, 1, LANES))
             for p in range(npairs)], axis=0)

    def store_pairs(val, off, npairs):
        for p in range(npairs):
            y_ref[:, :, off + p * LANES:off + (p + 1) * LANES] = val[p * grp:(p + 1) * grp]

    zg = z_ref[:, :, OFF_GATE:OFF_GATE + LANES]
    lane = lax.broadcasted_iota(jnp.int32, (1, 1, LANES), 2)
    tg = zg + prm_ref[0:1, :]
    tail = _log1pexp_negabs(tg)
    logsig = -(jnp.maximum(-tg, 0.0) + tail)
    splus = jnp.maximum(tg, 0.0) + tail
    lgv = -jnp.exp(prm_ref[1:2, :]) * splus
    gates = jnp.where(lane < GATE_MF, zg,
                      jnp.where(lane < GATE_GB, logsig,
                                jnp.where(lane < GATE_GA, _sigmoid(tg),
                                          jnp.where(lane < GATE_END, lgv, 0.0))))
    cs = jnp.stack([_mm_f32(tri, gates[g]) for g in range(grp)])
    gates_t = jnp.stack([gates[g].T for g in range(grp)])
    cs_t = jnp.stack([cs[g].T for g in range(grp)])

    q = pairs_of(z_ref, OFF_MQ, mp)
    k = pairs_of(z_ref, OFF_MK, mp) * QK_SCALE
    v = pairs_of(z_ref, OFF_MV, mp)
    ig_row = row_pairs(gates_t, GATE_MI, mp)
    b_row = row_pairs(cs_t, GATE_MF, mp)
    b_col_s = col_pairs(cs, GATE_MF, mp, c2)
    b_col_d = col_pairs(cs, GATE_MF, mp, LANES)
    ig_col_d = col_pairs(gates, GATE_MI, mp, LANES)
    m_prev_d = m_pr[...]
    m_prev_s = _expand(m_prev_d[:, :, 0:1], m_prev_d[:, :, HALF:HALF + 1], c2)
    dmat = jnp.where(incl, b_col_s - b_row + ig_row, -jnp.inf)
    rmax_a, rmax_b = _half_max(dmat)
    inter_s = b_col_s + m_prev_s
    inter_d = b_col_d + m_prev_d
    m_t_s = jnp.maximum(inter_s, _expand(rmax_a, rmax_b, c2))
    m_t_d = jnp.maximum(inter_d, _expand(rmax_a, rmax_b, LANES))
    w = jnp.exp(dmat - m_t_s)
    wi_d = jnp.exp(inter_d - m_t_d)
    wqk = w * _bmm_nt(q, _block_diag(k))
    c_st = c_bd[...]
    n_st = n_pr[...]
    num = _bmm(wqk, _block_diag(v)) + wi_d * _bmm(q, c_st)
    ws_a, ws_b = _half_sum(wqk)
    qn_a, qn_b = _half_sum(q * n_st)
    den = _expand(ws_a, ws_b, LANES) + wi_d * _expand(qn_a, qn_b, LANES)
    hm = num / jnp.maximum(jnp.abs(den), 1.0)
    m_new_d = m_t_d[:, c - 1:c, :]
    b_last_d = b_col_d[:, c - 1:c, :]
    ws = jnp.exp(b_last_d - b_col_d + ig_col_d - m_new_d)
    decay_d = jnp.exp(b_last_d + m_prev_d - m_new_d)
    kw = k * ws
    c_bd[...] = decay_d * c_st + jnp.where(same_head, _bmm_tn(kw, v), 0.0)
    n_pr[...] = decay_d * n_st + jnp.sum(kw, axis=1, keepdims=True)
    m_pr[...] = m_new_d
    store_pairs(_pair_rmsnorm_gate(hm, norm_weights(0, mp), _sigmoid(pairs_of(z_ref, OFF_MO, mp))), 0, mp)

    xbuf[:, SUBLANES:SUBLANES + c, :] = z_ref[:, :, OFF_GQKV:OFF_GQKV + 3 * G_W]
    pre = cw_ref[0:1, :] * xbuf[:, first:first + c, :]
    for j in range(1, CONV_W):
        pre = pre + cw_ref[j:j + 1, :] * xbuf[:, first + j:first + j + c, :]
    conv = pre * _sigmoid(pre)
    cv1_ref[...] = xbuf[:, c + first:c + SUBLANES, :]
    xbuf[:, 0:SUBLANES, :] = xbuf[:, c:c + SUBLANES, :]

    gq = pairs_of(conv, 0, gp)
    gk = pairs_of(conv, G_W, gp)
    gv = pairs_of(conv, 2 * G_W, gp)
    qq_a, qq_b = _half_sum(gq * gq)
    kk_a, kk_b = _half_sum(gk * gk)
    gq = gq * lax.rsqrt(_expand(qq_a, qq_b, LANES) + EPS) * QK_SCALE
    gk = gk * lax.rsqrt(_expand(kk_a, kk_b, LANES) + EPS)
    beta_d = col_pairs(gates, GATE_GB, gp, LANES)
    g_col_s = col_pairs(cs, GATE_GA, gp, c2)
    g_col_d = col_pairs(cs, GATE_GA, gp, LANES)
    g_row = row_pairs(cs_t, GATE_GA, gp)
    ediff = jnp.exp(jnp.where(incl, g_col_s - g_row, 0.0))
    kb = gk * beta_d
    gk_bd = _block_diag(gk)
    lm = jnp.where(strict, _bmm_nt(kb, gk_bd) * ediff, 0.0)
    tinv = None
    s = 1
    while s < c:
        same = (row // (2 * s)) == (col // (2 * s))
        off = same & ((row % (2 * s)) >= s) & ((col % (2 * s)) < s)
        coff = jnp.where(off, lm, 0.0)
        if tinv is None:
            tinv = eye_c - coff
        else:
            tinv = tinv - _bmm(_bmm(tinv, _block_diag(coff)), _block_diag(tinv))
        s *= 2
    eg_d = jnp.exp(g_col_d)
    u = _bmm(tinv, _block_diag(gv * beta_d))
    wk = _bmm(tinv, _block_diag(kb * eg_d))
    s_st = s_bd[...]
    v_new = u - _bmm(wk, s_st)
    attn = jnp.where(incl, _bmm_nt(gq, gk_bd) * ediff, 0.0)
    o = _bmm(gq * eg_d, s_st) + _bmm(attn, _block_diag(v_new))
    gl_d = g_col_d[:, c - 1:c, :]
    s_bd[...] = s_st * jnp.exp(gl_d) + jnp.where(same_head, _bmm_tn(gk * jnp.exp(gl_d - g_col_d), v_new), 0.0)
    gg = pairs_of(z_ref, OFF_GG, gp)
    store_pairs(_pair_rmsnorm_gate(o, norm_weights(M_W, gp), gg * _sigmoid(gg)), M_W, gp)

    hf = z_ref[:, :, OFF_HF:OFF_HF + H_W]
    ls = -(jnp.maximum(-hf, 0.0) + _log1pexp_negabs(hf))
    if layer == 0:
        lf = ls
        kk = _sigmoid(-hf)
    else:
        lg = lbl_ref[...]
        ex = jnp.exp(lg - jnp.max(lg, axis=0, keepdims=True))
        sm = ex / jnp.sum(ex, axis=0, keepdims=True)
        cum = sm[0:1, :]
        for j in range(1, layer + 1):
            cum = cum + sm[j:j + 1, :]
        lb = cum - sm[0:1, :]
        a = jnp.log(lb)
        b = jnp.log1p(-lb) + ls
        lf = jnp.maximum(a, b) + _log1pexp_negabs(a - b)
        kk = (1.0 - lb) * _sigmoid(-hf)
    gcum = jnp.stack([_mm_f32(tri, lf[g]) for g in range(grp)])
    nh = hp * grp
    trow = lax.broadcasted_iota(jnp.int32, (sb, LANES), 0)
    q = pairs_of(z_ref, OFF_HQ, hp)
    k = pairs_of(kk, 0, hp)
    v = pairs_of(z_ref, OFF_HI, hp)
    gh = pairs_of(gcum, 0, hp)
    s_st = h_bd[...]
    inter = _bmm(q * jnp.exp(gh), s_st)
    blocks = []
    for blk in range(c // sb):
        r0 = blk * sb
        qi = q[:, r0:r0 + sb]
        ki = k[:, r0:r0 + sb]
        vi = v[:, r0:r0 + sb]
        gi = gh[:, r0:r0 + sb]
        prods = []
        for s in range(sb):
            e = jnp.exp(jnp.where(trow >= s, gi - gi[:, s:s + 1], -jnp.inf))
            prods.append(qi * ki[:, s:s + 1] * e)
        stacked = jnp.concatenate(prods, axis=1).reshape(nh * sb * sb, LANES)
        rs = _mm(stacked, ones_bd).reshape(nh, sb * sb, LANES)
        oi = rs[:, 0:sb] * vi[:, 0:1]
        for s in range(1, sb):
            oi = oi + rs[:, s * sb:(s + 1) * sb] * vi[:, s:s + 1]
        if blk > 0:
            ref = gh[:, r0:r0 + 1]
            qt = qi * jnp.exp(gi - ref)
            kt = k[:, 0:r0] * jnp.exp(ref - gh[:, 0:r0])
            oi = oi + _bmm(_bmm_nt(qt, _block_diag(kt)), _block_diag(v[:, 0:r0]))
        blocks.append(oi)
    o = inter + (blocks[0] if len(blocks) == 1 else jnp.concatenate(blocks, axis=1))
    gl = gh[:, c - 1:c]
    decay_col = jnp.sum(eye_128 * jnp.exp(gl), axis=2, keepdims=True)
    h_bd[...] = decay_col * s_st + jnp.where(same_head, _bmm_tn(k * jnp.exp(gl - gh), v), 0.0)
    hg = pairs_of(z_ref, OFF_HG, hp)
    store_pairs(_pair_rmsnorm_gate(o, norm_weights(M_W + G_W, hp), hg * _sigmoid(hg)), M_W + G_W, hp)

    @pl.when(step == last)
    def _():
        store_bd(c1_ref, c_bd, mp)
        store_bd(s1_ref, s_bd, gp)
        store_bd(h1_ref, h_bd, hp)
        for p in range(mp):
            n1_ref[:, 2 * p:2 * p + 1, :] = n_pr[p * grp:(p + 1) * grp, :, 0:HALF]
            n1_ref[:, 2 * p + 1:2 * p + 2, :] = n_pr[p * grp:(p + 1) * grp, :, HALF:LANES]
            m1_ref[:, :, 2 * p:2 * p + 1] = m_pr[p * grp:(p + 1) * grp, :, 0:1]
            m1_ref[:, :, 2 * p + 1:2 * p + 2] = m_pr[p * grp:(p + 1) * grp, :, HALF:HALF + 1]


def _mixer_call(layer, z, prm, cw, lbl, wn, init, init_layer, acc, c, sb, grp):
    nb, seq, _ = z.shape
    nchunks = seq // c
    kern = functools.partial(_mixer_kernel, layer, c, sb, grp)
    const2 = lambda b, n: (0, 0)
    st5 = lambda b, n: (init_layer, b, 0, 0, 0)
    st4 = lambda b, n: (init_layer, b, 0, 0)
    o5 = lambda b, n: (layer, b, 0, 0, 0)
    o4 = lambda b, n: (layer, b, 0, 0)
    state_blocks = ((grp, M_HEADS, HEAD_DIM, HEAD_DIM), (grp, M_HEADS, HEAD_DIM), (grp, 1, M_HEADS),
                    (grp, G_HEADS, HEAD_DIM, HEAD_DIM), (grp, CONV_W - 1, 3 * G_W),
                    (grp, H_HEADS, HEAD_DIM, HEAD_DIM))
    n_in = 5 + len(state_blocks)
    return pl.pallas_call(
        kern,
        grid=(nb // grp, nchunks),
        in_specs=[
            pl.BlockSpec((grp, c, Z_W), lambda b, n: (b, n, 0)),
            pl.BlockSpec(prm.shape, const2),
            pl.BlockSpec(cw.shape, const2),
            pl.BlockSpec(lbl.shape, const2),
            pl.BlockSpec(wn.shape, const2),
        ] + [pl.BlockSpec((None,) + blk, st5 if len(blk) == 4 else st4) for blk in state_blocks]
          + [pl.BlockSpec(memory_space=pl.ANY) for _ in state_blocks],
        out_specs=[pl.BlockSpec((grp, c, D_MIX), lambda b, n: (b, n, 0))]
                  + [pl.BlockSpec((None,) + blk, o5 if len(blk) == 4 else o4) for blk in state_blocks],
        out_shape=[jax.ShapeDtypeStruct((nb, seq, D_MIX), F32)]
                  + [jax.ShapeDtypeStruct(a.shape, F32) for a in acc],
        input_output_aliases={n_in + i: 1 + i for i in range(len(state_blocks))},
        scratch_shapes=[
            pltpu.VMEM((grp, SUBLANES + c, 3 * G_W), F32),
            pltpu.VMEM((M_HEADS // 2 * grp, LANES, LANES), F32),
            pltpu.VMEM((M_HEADS // 2 * grp, 1, LANES), F32),
            pltpu.VMEM((M_HEADS // 2 * grp, 1, LANES), F32),
            pltpu.VMEM((G_HEADS // 2 * grp, LANES, LANES), F32),
            pltpu.VMEM((H_HEADS // 2 * grp, LANES, LANES), F32),
        ],
        compiler_params=pltpu.CompilerParams(
            dimension_semantics=("parallel", "arbitrary"), vmem_limit_bytes=VMEM_LIMIT_BYTES),
        name="mixer",
    )(z, prm, cw, lbl, wn, *init, *acc)


def _permute_in_proj(w_in, b_in):
    o = 0
    pieces = {}
    for name, width in (("mq", M_W), ("mk", M_W), ("mv", M_W), ("mo", M_W), ("mi", M_HEADS), ("mf", M_HEADS),
                        ("gqkv", 3 * G_W), ("gg", G_W), ("gb", G_HEADS), ("ga", G_HEADS),
                        ("hq", H_W), ("hf", H_W), ("hi", H_W), ("hg", H_W)):
        pieces[name] = (o, o + width)
        o += width
    order = ("mq", "mk", "mv", "mo", "gqkv", "gg", "hq", "hf", "hi", "hg", "mi", "mf", "gb", "ga")

    def perm(a):
        cols = [a[..., pieces[n][0]:pieces[n][1]] for n in order]
        pad = jnp.zeros(a.shape[:-1] + (LANES - GATE_END,), a.dtype)
        return jnp.concatenate(cols + [pad], axis=-1)

    return perm(w_in).astype(BF16), perm(b_in).reshape(DEPTH, 1, Z_W)


def _gate_params(mlstm_f_bias, gdn_A_log, gdn_dt_bias):
    prm = jnp.zeros((DEPTH, SUBLANES, LANES), F32)
    prm = prm.at[:, 0, GATE_MF:GATE_MF + M_HEADS].set(mlstm_f_bias.astype(F32))
    prm = prm.at[:, 0, GATE_GA:GATE_GA + G_HEADS].set(gdn_dt_bias.astype(F32))
    prm = prm.at[:, 1, GATE_GA:GATE_GA + G_HEADS].set(gdn_A_log.astype(F32))
    return prm


def _trunk(x, mod, mod_row0, init, init_is_shared, params, c, sb, grp, in_bt, ffn_bt):
    (npre_mix, npost_mix, w_in, b_in, prm, conv_w, lb_logits, out_norm, w_out, npre_ffn, npost_ffn, w_ff1,
     w_ff2) = params
    nb, seq, _ = x.shape
    acc = tuple(jnp.zeros((DEPTH,) + a.shape[1:], F32) for a in init)
    for l in range(DEPTH):
        z = _inproj_call(x, mod, l, mod_row0, npre_mix[l], w_in[l], b_in[l], *in_bt)
        outs = _mixer_call(l, z.reshape(nb, seq, Z_W), prm[l], conv_w[l], lb_logits, out_norm[l],
                           init, 0 if init_is_shared else l, acc, c, sb, grp)
        ymix = outs[0].reshape(nb * seq, D_MIX)
        acc = tuple(outs[1:])
        x = _ffn_call(x, ymix, mod, l, mod_row0, w_out[l], w_ff1[l], w_ff2[l], npost_mix[l], npre_ffn[l],
                      npost_ffn[l], *ffn_bt)
    c1, n1, m1, s1, cv1, h1 = acc
    return x, (c1, n1, m1.reshape(DEPTH, nb, M_HEADS), s1, cv1, h1)


def kernel(x_prompt, x_sample, state_mlstm_C, state_mlstm_n, state_mlstm_m, state_gdn_S, state_gdn_conv,
           state_hgrn_S, c_prompt, c_sample, w_ada, b_ada, norm_pre_mix, norm_post_mix, w_in, b_in,
           mlstm_f_bias, gdn_conv_w, gdn_A_log, gdn_dt_bias, hgrn_lb_logits, mix_out_norm, w_out,
           norm_pre_ffn, norm_post_ffn, w_ff1, w_ff2):
    bp, seq_p, _ = x_prompt.shape
    bs, seq_s, _ = x_sample.shape

    mod = _ada_call(jnp.concatenate([c_sample, c_prompt], axis=0), w_ada, b_ada)

    w_in_p, b_in_p = _permute_in_proj(w_in, b_in)
    row = lambda a: a.reshape(DEPTH, 1, a.shape[-1]).astype(F32)
    params = (row(norm_pre_mix), row(norm_post_mix), w_in_p, b_in_p,
              _gate_params(mlstm_f_bias, gdn_A_log, gdn_dt_bias), gdn_conv_w.astype(F32),
              hgrn_lb_logits.astype(F32), row(mix_out_norm), w_out.astype(BF16), row(norm_pre_ffn),
              row(norm_post_ffn), w_ff1.astype(BF16), w_ff2.astype(BF16))

    zeros = (jnp.zeros((1, bp, M_HEADS, HEAD_DIM, HEAD_DIM), F32),
             jnp.zeros((1, bp, M_HEADS, HEAD_DIM), F32),
             jnp.zeros((1, bp, 1, M_HEADS), F32),
             jnp.zeros((1, bp, G_HEADS, HEAD_DIM, HEAD_DIM), F32),
             jnp.zeros((1, bp, CONV_W - 1, 3 * G_W), F32),
             jnp.zeros((1, bp, H_HEADS, HEAD_DIM, HEAD_DIM), F32))
    past = (state_mlstm_C.astype(F32), state_mlstm_n.astype(F32),
            state_mlstm_m.astype(F32).reshape(DEPTH, bs, 1, M_HEADS), state_gdn_S.astype(F32),
            state_gdn_conv.astype(F32), state_hgrn_S.astype(F32))

    y_p, st_p = _trunk(x_prompt, mod, bs, zeros, True, params, PROMPT_CHUNK, HGRN_SUBBLOCK, PROMPT_GROUP,
                       (1, IN_TILE), (1, FFN_TILE))
    y_s, st_s = _trunk(x_sample, mod, 0, past, False, params, seq_s, min(seq_s, HGRN_SUBBLOCK), SAMPLE_GROUP,
                       (IN_TILE // seq_s, seq_s), (FFN_TILE // seq_s, seq_s))
    return (y_p, y_s) + st_p + st_s
```

```python
import functools

import jax
import jax.numpy as jnp
from jax import lax
from jax.experimental import pallas as pl
from jax.experimental.pallas import tpu as pltpu

F32 = jnp.float32
BF16 = jnp.bfloat16

D_MODEL = 1024
DEPTH = 4
HEAD_DIM = 64
M_HEADS = 6
G_HEADS = 6
H_HEADS = 4
M_W = M_HEADS * HEAD_DIM
G_W = G_HEADS * HEAD_DIM
H_W = H_HEADS * HEAD_DIM
D_MIX = M_W + G_W + H_W
CONV_W = 4
D_FF = 4 * D_MODEL
EPS = 1e-6
QK_SCALE = HEAD_DIM ** -0.5

LANES = 128
SUBLANES = 8
VMEM_LIMIT_BYTES = 56 * 1024 * 1024

OFF_MQ = 0
OFF_MK = OFF_MQ + M_W
OFF_MV = OFF_MK + M_W
OFF_MO = OFF_MV + M_W
OFF_GQKV = OFF_MO + M_W
OFF_GG = OFF_GQKV + 3 * G_W
OFF_HQ = OFF_GG + G_W
OFF_HF = OFF_HQ + H_W
OFF_HI = OFF_HF + H_W
OFF_HG = OFF_HI + H_W
OFF_GATE = OFF_HG + H_W
Z_W = OFF_GATE + LANES
GATE_MI = 0
GATE_MF = GATE_MI + M_HEADS
GATE_GB = GATE_MF + M_HEADS
GATE_GA = GATE_GB + G_HEADS
GATE_END = GATE_GA + G_HEADS

PROMPT_CHUNK = 64
HGRN_SUBBLOCK = 8
PROMPT_GROUP = 8
SAMPLE_GROUP = 16
IN_TILE = 512
FFN_TILE = 512
FF_BLOCK = 2048
ADA_BLOCK = 1536


def _mm(a, b):
    return jnp.dot(a.astype(BF16), b.astype(BF16), preferred_element_type=F32)


def _mm_f32(a, b):
    return jnp.dot(a, b, preferred_element_type=F32, precision=lax.Precision.HIGHEST)


def _bmm(a, b):
    return jnp.einsum("nmk,nkp->nmp", a.astype(BF16), b.astype(BF16), preferred_element_type=F32)


def _bmm_nt(a, b):
    return jnp.einsum("nmk,npk->nmp", a.astype(BF16), b.astype(BF16), preferred_element_type=F32)


def _bmm_tn(a, b):
    return jnp.einsum("nsk,nsp->nkp", a.astype(BF16), b.astype(BF16), preferred_element_type=F32)


def _sigmoid(x):
    return 0.5 * jnp.tanh(0.5 * x) + 0.5


def _log1pexp_negabs(x):
    return jnp.log1p(jnp.exp(-jnp.abs(x)))


def _rms(x, w):
    ms = jnp.mean(x * x, axis=-1, keepdims=True)
    return x * lax.rsqrt(ms + EPS) * w


def _ada_kernel(c_ref, w_ref, b_ref, o_ref):
    mod = jnp.dot(c_ref[...].astype(BF16), w_ref[...].astype(BF16), preferred_element_type=F32) + b_ref[...]
    o_ref[...] = mod[:, None, :]


def _ada_call(c_all, w_ada, b_ada):
    rows = c_all.shape[0]
    nblk = (6 * D_MODEL) // ADA_BLOCK
    return pl.pallas_call(
        _ada_kernel,
        grid=(DEPTH, nblk),
        in_specs=[
            pl.BlockSpec((rows, D_MODEL), lambda l, j: (0, 0)),
            pl.BlockSpec((None, D_MODEL, ADA_BLOCK), lambda l, j: (l, 0, j)),
            pl.BlockSpec((None, 1, ADA_BLOCK), lambda l, j: (l, 0, j)),
        ],
        out_specs=pl.BlockSpec((None, rows, 1, ADA_BLOCK), lambda l, j: (l, 0, 0, j)),
        out_shape=jax.ShapeDtypeStruct((DEPTH, rows, 1, 6 * D_MODEL), F32),
        compiler_params=pltpu.CompilerParams(
            dimension_semantics=("parallel", "parallel"), vmem_limit_bytes=VMEM_LIMIT_BYTES),
        name="adaln",
    )(c_all, w_ada, b_ada.reshape(DEPTH, 1, 6 * D_MODEL))


def _inproj_kernel(x_ref, mod_ref, npre_ref, w_ref, b_ref, z_ref):
    bb, t, _ = x_ref.shape
    mod = mod_ref[...]
    sh1 = mod[:, :, 0:D_MODEL]
    sc1 = mod[:, :, D_MODEL:2 * D_MODEL]
    h = _rms(x_ref[...], npre_ref[...]) * (1.0 + sc1) + sh1
    hb = h.reshape(bb * t, D_MODEL).astype(BF16)
    z_ref[...] = jnp.dot(hb, w_ref[...], preferred_element_type=F32) + b_ref[...]


def _inproj_call(x, mod, layer, mod_row0, npre, w, b, bb, t):
    nb, seq, _ = x.shape
    grid = (nb // bb, seq // t)
    nseq = seq // t
    mod_blk0 = mod_row0 // bb
    return pl.pallas_call(
        _inproj_kernel,
        grid=grid,
        in_specs=[
            pl.BlockSpec((bb, t, D_MODEL), lambda i, j: (i, j, 0)),
            pl.BlockSpec((None, bb, 1, 6 * D_MODEL), lambda i, j: (layer, mod_blk0 + i, 0, 0)),
            pl.BlockSpec((1, D_MODEL), lambda i, j: (0, 0)),
            pl.BlockSpec((D_MODEL, Z_W), lambda i, j: (0, 0)),
            pl.BlockSpec((1, Z_W), lambda i, j: (0, 0)),
        ],
        out_specs=pl.BlockSpec((bb * t, Z_W), lambda i, j: (i * nseq + j, 0)),
        out_shape=jax.ShapeDtypeStruct((nb * seq, Z_W), F32),
        compiler_params=pltpu.CompilerParams(
            dimension_semantics=("parallel", "parallel"), vmem_limit_bytes=VMEM_LIMIT_BYTES),
        name="inproj",
    )(x, mod, npre, w, b)


def _ffn_prologue(slot, x_ref, ym_ref, mod_ref, wout_ref, npost_ref, npre_ref, x1_s, h2_s, g2_s):
    bb, t, _ = x_ref.shape
    mod = mod_ref[...]
    g1 = mod[:, :, 2 * D_MODEL:3 * D_MODEL]
    sh2 = mod[:, :, 3 * D_MODEL:4 * D_MODEL]
    sc2 = mod[:, :, 4 * D_MODEL:5 * D_MODEL]
    y = jnp.dot(ym_ref[...].astype(BF16), wout_ref[...], preferred_element_type=F32)
    x1 = x_ref[...] + g1 * _rms(y.reshape(bb, t, D_MODEL), npost_ref[...])
    h2 = _rms(x1, npre_ref[...]) * (1.0 + sc2) + sh2
    x1_s[slot] = x1
    h2_s[slot] = h2.reshape(bb * t, D_MODEL).astype(BF16)
    g2_s[slot] = mod[:, :, 5 * D_MODEL:6 * D_MODEL]


def _ffn_half(slot, h2_s, w1_ref, w2_ref):
    a = jnp.dot(h2_s[slot], w1_ref[...], preferred_element_type=F32)
    a = jnp.square(jnp.maximum(a, 0.0))
    return jnp.dot(a.astype(BF16), w2_ref[...], preferred_element_type=F32)


def _ffn_kernel(x_ref, ym_ref, mod_ref, wout_ref, w1_ref, w2_ref, npost_ref, npre_ref, npostf_ref,
                o_ref, x1_s, h2_s, g2_s, acc_s):
    t = pl.program_id(0)
    j = pl.program_id(1)
    bb, tt, _ = x_ref.shape
    cur = t % 2
    prev = 1 - cur
    pro = functools.partial(_ffn_prologue, x_ref=x_ref, ym_ref=ym_ref, mod_ref=mod_ref, wout_ref=wout_ref,
                            npost_ref=npost_ref, npre_ref=npre_ref, x1_s=x1_s, h2_s=h2_s, g2_s=g2_s)

    @pl.when((t == 0) & (j == 0))
    def _():
        pro(cur)

    @pl.when((t > 0) & (j == 0))
    def _():
        acc_s[...] = _ffn_half(prev, h2_s, w1_ref, w2_ref)
        pro(cur)

    @pl.when((t > 0) & (j == 1))
    def _():
        f = (acc_s[...] + _ffn_half(prev, h2_s, w1_ref, w2_ref)).reshape(bb, tt, D_MODEL)
        o_ref[...] = x1_s[prev] + g2_s[prev] * _rms(f, npostf_ref[...])


def _ffn_call(x, ymix, mod, layer, mod_row0, wout, w1, w2, npost, npre, npostf, bb, t):
    nb, seq, _ = x.shape
    nseq = seq // t
    ntiles = (nb // bb) * nseq
    assert D_FF == 2 * FF_BLOCK
    mod_blk0 = mod_row0 // bb
    const2 = lambda i, j: (0, 0)

    def tile_in(i):
        return jnp.minimum(i, ntiles - 1)

    def tile_out(i):
        return jnp.maximum(i - 1, 0)

    return pl.pallas_call(
        _ffn_kernel,
        grid=(ntiles + 1, 2),
        in_specs=[
            pl.BlockSpec((bb, t, D_MODEL), lambda i, j: (tile_in(i) // nseq, tile_in(i) % nseq, 0)),
            pl.BlockSpec((bb * t, D_MIX), lambda i, j: (tile_in(i), 0)),
            pl.BlockSpec((None, bb, 1, 6 * D_MODEL), lambda i, j: (layer, mod_blk0 + tile_in(i) // nseq, 0, 0)),
            pl.BlockSpec((D_MIX, D_MODEL), const2),
            pl.BlockSpec((D_MODEL, FF_BLOCK), lambda i, j: (0, j)),
            pl.BlockSpec((FF_BLOCK, D_MODEL), lambda i, j: (j, 0)),
            pl.BlockSpec((1, D_MODEL), const2),
            pl.BlockSpec((1, D_MODEL), const2),
            pl.BlockSpec((1, D_MODEL), const2),
        ],
        out_specs=pl.BlockSpec((bb, t, D_MODEL), lambda i, j: (tile_out(i) // nseq, tile_out(i) % nseq, 0)),
        out_shape=jax.ShapeDtypeStruct(x.shape, F32),
        scratch_shapes=[
            pltpu.VMEM((2, bb, t, D_MODEL), F32),
            pltpu.VMEM((2, bb * t, D_MODEL), BF16),
            pltpu.VMEM((2, bb, 1, D_MODEL), F32),
            pltpu.VMEM((bb * t, D_MODEL), F32),
        ],
        compiler_params=pltpu.CompilerParams(
            dimension_semantics=("arbitrary", "arbitrary"), vmem_limit_bytes=VMEM_LIMIT_BYTES),
        name="outffn",
    )(x, ymix, mod, wout, w1, w2, npost, npre, npostf)


HALF = LANES // 2
assert HALF == HEAD_DIM


def _lane_low(width):
    return lax.broadcasted_iota(jnp.int32, (1, 1, width), 2) < width // 2


def _expand(col_a, col_b, width):
    return jnp.where(_lane_low(width), col_a, col_b)


def _half_sum(x):
    low = _lane_low(x.shape[2])
    return (jnp.sum(jnp.where(low, x, 0.0), axis=2, keepdims=True),
            jnp.sum(jnp.where(low, 0.0, x), axis=2, keepdims=True))


def _half_max(x):
    low = _lane_low(x.shape[2])
    return (jnp.max(jnp.where(low, x, -jnp.inf), axis=2, keepdims=True),
            jnp.max(jnp.where(low, -jnp.inf, x), axis=2, keepdims=True))


def _block_diag(x):
    low = _lane_low(x.shape[2])
    bf16_rows = 2 * SUBLANES
    xb = x.astype(BF16) if x.shape[1] % bf16_rows == 0 else x
    zero = jnp.zeros_like(xb)
    return jnp.concatenate([jnp.where(low, xb, zero), jnp.where(low, zero, xb)], axis=1).astype(BF16)


def _pair_rmsnorm_gate(o, wn, gate):
    sa, sb = _half_sum(o * o)
    ms = _expand(sa, sb, LANES) * (1.0 / HEAD_DIM)
    return o * lax.rsqrt(ms + EPS) * wn * gate


def _load_block_diag(dst, src, npairs, grp):
    dst[...] = jnp.zeros_like(dst)
    for p in range(npairs):
        dst[p * grp:(p + 1) * grp, 0:HALF, 0:HALF] = src[:, 2 * p]
        dst[p * grp:(p + 1) * grp, HALF:LANES, HALF:LANES] = src[:, 2 * p + 1]


def _store_block_diag(dst, src, npairs, grp):
    for p in range(npairs):
        dst[:, 2 * p] = src[p * grp:(p + 1) * grp, 0:HALF, 0:HALF]
        dst[:, 2 * p + 1] = src[p * grp:(p + 1) * grp, HALF:LANES, HALF:LANES]


_CONV_FIRST = SUBLANES - (CONV_W - 1)


def _mixer_init(grp, init_refs, scratch):
    c0_ref, n0_ref, m0_ref, s0_ref, cv0_ref, h0_ref = init_refs
    xbuf, c_bd, n_pr, m_pr, s_bd, h_bd = scratch
    _load_block_diag(c_bd, c0_ref, M_HEADS // 2, grp)
    _load_block_diag(s_bd, s0_ref, G_HEADS // 2, grp)
    _load_block_diag(h_bd, h0_ref, H_HEADS // 2, grp)
    for p in range(M_HEADS // 2):
        n_pr[p * grp:(p + 1) * grp] = jnp.concatenate(
            [n0_ref[:, 2 * p:2 * p + 1, :], n0_ref[:, 2 * p + 1:2 * p + 2, :]], axis=2)
        m_pr[p * grp:(p + 1) * grp] = _expand(m0_ref[:, :, 2 * p:2 * p + 1], m0_ref[:, :, 2 * p + 1:2 * p + 2],
                                              LANES)
    xbuf[:, _CONV_FIRST:SUBLANES, :] = cv0_ref[...]


def _mixer_finalize(grp, out_refs, scratch):
    c1_ref, n1_ref, m1_ref, s1_ref, cv1_ref, h1_ref = out_refs
    xbuf, c_bd, n_pr, m_pr, s_bd, h_bd = scratch
    _store_block_diag(c1_ref, c_bd, M_HEADS // 2, grp)
    _store_block_diag(s1_ref, s_bd, G_HEADS // 2, grp)
    _store_block_diag(h1_ref, h_bd, H_HEADS // 2, grp)
    for p in range(M_HEADS // 2):
        n1_ref[:, 2 * p:2 * p + 1, :] = n_pr[p * grp:(p + 1) * grp, :, 0:HALF]
        n1_ref[:, 2 * p + 1:2 * p + 2, :] = n_pr[p * grp:(p + 1) * grp, :, HALF:LANES]
        m1_ref[:, :, 2 * p:2 * p + 1] = m_pr[p * grp:(p + 1) * grp, :, 0:1]
        m1_ref[:, :, 2 * p + 1:2 * p + 2] = m_pr[p * grp:(p + 1) * grp, :, HALF:HALF + 1]
    cv1_ref[...] = xbuf[:, _CONV_FIRST:SUBLANES, :]


def _mixer_body(layer, c, sb, grp, z_ref, prm_ref, cw_ref, lbl_ref, wn_ref, y_ref, scratch):
    xbuf, c_bd, n_pr, m_pr, s_bd, h_bd = scratch
    first = _CONV_FIRST
    mp, gp, hp = M_HEADS // 2, G_HEADS // 2, H_HEADS // 2
    c2 = 2 * c

    row = lax.broadcasted_iota(jnp.int32, (c, c2), 0)
    col = lax.broadcasted_iota(jnp.int32, (c, c2), 1) % c
    incl = row >= col
    strict = row > col
    eye_c = (row == col).astype(F32)
    rc = lax.broadcasted_iota(jnp.int32, (c, c), 0)
    cc = lax.broadcasted_iota(jnp.int32, (c, c), 1)
    tri = (rc >= cc).astype(F32)
    r128 = lax.broadcasted_iota(jnp.int32, (LANES, LANES), 0)
    c128 = lax.broadcasted_iota(jnp.int32, (LANES, LANES), 1)
    same_head = (r128 < HALF) == (c128 < HALF)
    ones_bd = same_head.astype(F32)
    eye_128 = (r128 == c128).astype(F32)

    def pairs_of(ref_or_val, off, npairs):
        return jnp.concatenate(
            [ref_or_val[:, :, off + p * LANES:off + (p + 1) * LANES] for p in range(npairs)], axis=0)

    def col_pairs(a, off, npairs, width):
        return jnp.concatenate(
            [_expand(a[:, :, off + 2 * p:off + 2 * p + 1], a[:, :, off + 2 * p + 1:off + 2 * p + 2], width)
             for p in range(npairs)], axis=0)

    def row_pairs(a_t, off, npairs):
        return jnp.concatenate(
            [jnp.concatenate([a_t[:, off + 2 * p:off + 2 * p + 1, :], a_t[:, off + 2 * p + 1:off + 2 * p + 2, :]],
                             axis=2) for p in range(npairs)], axis=0)

    def norm_weights(off, npairs):
        return jnp.concatenate(
            [jnp.broadcast_to(wn_ref[:, off + p * LANES:off + (p + 1) * LANES][None], (grp, 1, LANES))
             for p in range(npairs)], axis=0)

    def store_pairs(val, off, npairs):
        for p in range(npairs):
            y_ref[:, :, off + p * LANES:off + (p + 1) * LANES] = val[p * grp:(p + 1) * grp]

    zg = z_ref[:, :, OFF_GATE:OFF_GATE + LANES]
    lane = lax.broadcasted_iota(jnp.int32, (1, 1, LANES), 2)
    tg = zg + prm_ref[0:1, :]
    tail = _log1pexp_negabs(tg)
    logsig = -(jnp.maximum(-tg, 0.0) + tail)
    splus = jnp.maximum(tg, 0.0) + tail
    lgv = -jnp.exp(prm_ref[1:2, :]) * splus
    gates = jnp.where(lane < GATE_MF, zg,
                      jnp.where(lane < GATE_GB, logsig,
                                jnp.where(lane < GATE_GA, _sigmoid(tg),
                                          jnp.where(lane < GATE_END, lgv, 0.0))))
    cs = jnp.stack([_mm_f32(tri, gates[g]) for g in range(grp)])
    gates_t = jnp.stack([gates[g].T for g in range(grp)])
    cs_t = jnp.stack([cs[g].T for g in range(grp)])

    q = pairs_of(z_ref, OFF_MQ, mp)
    k = pairs_of(z_ref, OFF_MK, mp) * QK_SCALE
    v = pairs_of(z_ref, OFF_MV, mp)
    ig_row = row_pairs(gates_t, GATE_MI, mp)
    b_row = row_pairs(cs_t, GATE_MF, mp)
    b_col_s = col_pairs(cs, GATE_MF, mp, c2)
    b_col_d = col_pairs(cs, GATE_MF, mp, LANES)
    ig_col_d = col_pairs(gates, GATE_MI, mp, LANES)
    m_prev_d = m_pr[...]
    m_prev_s = _expand(m_prev_d[:, :, 0:1], m_prev_d[:, :, HALF:HALF + 1], c2)
    dmat = jnp.where(incl, b_col_s - b_row + ig_row, -jnp.inf)
    rmax_a, rmax_b = _half_max(dmat)
    inter_s = b_col_s + m_prev_s
    inter_d = b_col_d + m_prev_d
    m_t_s = jnp.maximum(inter_s, _expand(rmax_a, rmax_b, c2))
    m_t_d = jnp.maximum(inter_d, _expand(rmax_a, rmax_b, LANES))
    w = jnp.exp(dmat - m_t_s)
    wi_d = jnp.exp(inter_d - m_t_d)
    wqk = w * _bmm_nt(q, _block_diag(k))
    c_st = c_bd[...]
    n_st = n_pr[...]
    num = _bmm(wqk, _block_diag(v)) + wi_d * _bmm(q, c_st)
    ws_a, ws_b = _half_sum(wqk)
    qn_a, qn_b = _half_sum(q * n_st)
    den = _expand(ws_a, ws_b, LANES) + wi_d * _expand(qn_a, qn_b, LANES)
    hm = num / jnp.maximum(jnp.abs(den), 1.0)
    m_new_d = m_t_d[:, c - 1:c, :]
    b_last_d = b_col_d[:, c - 1:c, :]
    ws = jnp.exp(b_last_d - b_col_d + ig_col_d - m_new_d)
    decay_d = jnp.exp(b_last_d + m_prev_d - m_new_d)
    kw = k * ws
    c_bd[...] = decay_d * c_st + jnp.where(same_head, _bmm_tn(kw, v), 0.0)
    n_pr[...] = decay_d * n_st + jnp.sum(kw, axis=1, keepdims=True)
    m_pr[...] = m_new_d
    store_pairs(_pair_rmsnorm_gate(hm, norm_weights(0, mp), _sigmoid(pairs_of(z_ref, OFF_MO, mp))), 0, mp)

    xbuf[:, SUBLANES:SUBLANES + c, :] = z_ref[:, :, OFF_GQKV:OFF_GQKV + 3 * G_W]
    pre = cw_ref[0:1, :] * xbuf[:, first:first + c, :]
    for j in range(1, CONV_W):
        pre = pre + cw_ref[j:j + 1, :] * xbuf[:, first + j:first + j + c, :]
    conv = pre * _sigmoid(pre)
    xbuf[:, 0:SUBLANES, :] = xbuf[:, c:c + SUBLANES, :]

    gq = pairs_of(conv, 0, gp)
    gk = pairs_of(conv, G_W, gp)
    gv = pairs_of(conv, 2 * G_W, gp)
    qq_a, qq_b = _half_sum(gq * gq)
    kk_a, kk_b = _half_sum(gk * gk)
    gq = gq * lax.rsqrt(_expand(qq_a, qq_b, LANES) + EPS) * QK_SCALE
    gk = gk * lax.rsqrt(_expand(kk_a, kk_b, LANES) + EPS)
    beta_d = col_pairs(gates, GATE_GB, gp, LANES)
    g_col_s = col_pairs(cs, GATE_GA, gp, c2)
    g_col_d = col_pairs(cs, GATE_GA, gp, LANES)
    g_row = row_pairs(cs_t, GATE_GA, gp)
    ediff = jnp.exp(jnp.where(incl, g_col_s - g_row, 0.0))
    kb = gk * beta_d
    gk_bd = _block_diag(gk)
    lm = jnp.where(strict, _bmm_nt(kb, gk_bd) * ediff, 0.0)
    tinv = None
    s = 1
    while s < c:
        same = (row // (2 * s)) == (col // (2 * s))
        off = same & ((row % (2 * s)) >= s) & ((col % (2 * s)) < s)
        coff = jnp.where(off, lm, 0.0)
        if tinv is None:
            tinv = eye_c - coff
        else:
            tinv = tinv - _bmm(_bmm(tinv, _block_diag(coff)), _block_diag(tinv))
        s *= 2
    eg_d = jnp.exp(g_col_d)
    u = _bmm(tinv, _block_diag(gv * beta_d))
    wk = _bmm(tinv, _block_diag(kb * eg_d))
    s_st = s_bd[...]
    v_new = u - _bmm(wk, s_st)
    attn = jnp.where(incl, _bmm_nt(gq, gk_bd) * ediff, 0.0)
    o = _bmm(gq * eg_d, s_st) + _bmm(attn, _block_diag(v_new))
    gl_d = g_col_d[:, c - 1:c, :]
    s_bd[...] = s_st * jnp.exp(gl_d) + jnp.where(same_head, _bmm_tn(gk * jnp.exp(gl_d - g_col_d), v_new), 0.0)
    gg = pairs_of(z_ref, OFF_GG, gp)
    store_pairs(_pair_rmsnorm_gate(o, norm_weights(M_W, gp), gg * _sigmoid(gg)), M_W, gp)

    hf = z_ref[:, :, OFF_HF:OFF_HF + H_W]
    ls = -(jnp.maximum(-hf, 0.0) + _log1pexp_negabs(hf))
    if layer == 0:
        lf = ls
        kk = _sigmoid(-hf)
    else:
        lg = lbl_ref[...]
        ex = jnp.exp(lg - jnp.max(lg, axis=0, keepdims=True))
        sm = ex / jnp.sum(ex, axis=0, keepdims=True)
        cum = sm[0:1, :]
        for j in range(1, layer + 1):
            cum = cum + sm[j:j + 1, :]
        lb = cum - sm[0:1, :]
        a = jnp.log(lb)
        b = jnp.log1p(-lb) + ls
        lf = jnp.maximum(a, b) + _log1pexp_negabs(a - b)
        kk = (1.0 - lb) * _sigmoid(-hf)
    gcum = jnp.stack([_mm_f32(tri, lf[g]) for g in range(grp)])
    nh = hp * grp
    trow = lax.broadcasted_iota(jnp.int32, (sb, LANES), 0)
    q = pairs_of(z_ref, OFF_HQ, hp)
    k = pairs_of(kk, 0, hp)
    v = pairs_of(z_ref, OFF_HI, hp)
    gh = pairs_of(gcum, 0, hp)
    s_st = h_bd[...]
    inter = _bmm(q * jnp.exp(gh), s_st)
    blocks = []
    for blk in range(c // sb):
        r0 = blk * sb
        qi = q[:, r0:r0 + sb]
        ki = k[:, r0:r0 + sb]
        vi = v[:, r0:r0 + sb]
        gi = gh[:, r0:r0 + sb]
        prods = []
        for s in range(sb):
            e = jnp.exp(jnp.where(trow >= s, gi - gi[:, s:s + 1], -jnp.inf))
            prods.append(qi * ki[:, s:s + 1] * e)
        stacked = jnp.concatenate(prods, axis=1).reshape(nh * sb * sb, LANES)
        rs = _mm(stacked, ones_bd).reshape(nh, sb * sb, LANES)
        oi = rs[:, 0:sb] * vi[:, 0:1]
        for s in range(1, sb):
            oi = oi + rs[:, s * sb:(s + 1) * sb] * vi[:, s:s + 1]
        if blk > 0:
            ref = gh[:, r0:r0 + 1]
            qt = qi * jnp.exp(gi - ref)
            kt = k[:, 0:r0] * jnp.exp(ref - gh[:, 0:r0])
            oi = oi + _bmm(_bmm_nt(qt, _block_diag(kt)), _block_diag(v[:, 0:r0]))
        blocks.append(oi)
    o = inter + (blocks[0] if len(blocks) == 1 else jnp.concatenate(blocks, axis=1))
    gl = gh[:, c - 1:c]
    decay_col = jnp.sum(eye_128 * jnp.exp(gl), axis=2, keepdims=True)
    h_bd[...] = decay_col * s_st + jnp.where(same_head, _bmm_tn(k * jnp.exp(gl - gh), v), 0.0)
    hg = pairs_of(z_ref, OFF_HG, hp)
    store_pairs(_pair_rmsnorm_gate(o, norm_weights(M_W + G_W, hp), hg * _sigmoid(hg)), M_W + G_W, hp)


def _mixer_kernel(layer, c, sb, grp,
                  z_ref, prm_ref, cw_ref, lbl_ref, wn_ref, c0_ref, n0_ref, m0_ref, s0_ref, cv0_ref, h0_ref,
                  y_ref, c1_ref, n1_ref, m1_ref, s1_ref, cv1_ref, h1_ref,
                  xbuf, c_bd, n_pr, m_pr, s_bd, h_bd):
    step = pl.program_id(1)
    scratch = (xbuf, c_bd, n_pr, m_pr, s_bd, h_bd)

    @pl.when(step == 0)
    def _():
        _mixer_init(grp, (c0_ref, n0_ref, m0_ref, s0_ref, cv0_ref, h0_ref), scratch)

    _mixer_body(layer, c, sb, grp, z_ref, prm_ref, cw_ref, lbl_ref, wn_ref, y_ref, scratch)

    @pl.when(step == pl.num_programs(1) - 1)
    def _():
        _mixer_finalize(grp, (c1_ref, n1_ref, m1_ref, s1_ref, cv1_ref, h1_ref), scratch)


def _mixer_call(layer, z, prm, cw, lbl, wn, states, c, sb, grp):
    nb, seq, _ = z.shape
    nchunks = seq // c
    kern = functools.partial(_mixer_kernel, layer, c, sb, grp)
    const2 = lambda b, n: (0, 0)
    st5 = lambda b, n: (layer, b, 0, 0, 0)
    st4 = lambda b, n: (layer, b, 0, 0)
    state_blocks = ((grp, M_HEADS, HEAD_DIM, HEAD_DIM), (grp, M_HEADS, HEAD_DIM), (grp, 1, M_HEADS),
                    (grp, G_HEADS, HEAD_DIM, HEAD_DIM), (grp, CONV_W - 1, 3 * G_W),
                    (grp, H_HEADS, HEAD_DIM, HEAD_DIM))
    state_specs = [pl.BlockSpec((None,) + blk, st5 if len(blk) == 4 else st4) for blk in state_blocks]
    n_in = 5
    return pl.pallas_call(
        kern,
        grid=(nb // grp, nchunks),
        in_specs=[
            pl.BlockSpec((grp, c, Z_W), lambda b, n: (b, n, 0)),
            pl.BlockSpec(prm.shape, const2),
            pl.BlockSpec(cw.shape, const2),
            pl.BlockSpec(lbl.shape, const2),
            pl.BlockSpec(wn.shape, const2),
        ] + state_specs,
        out_specs=[pl.BlockSpec((grp, c, D_MIX), lambda b, n: (b, n, 0))] + state_specs,
        out_shape=[jax.ShapeDtypeStruct((nb, seq, D_MIX), F32)]
                  + [jax.ShapeDtypeStruct(a.shape, F32) for a in states],
        input_output_aliases={n_in + i: 1 + i for i in range(len(state_blocks))},
        scratch_shapes=[
            pltpu.VMEM((grp, SUBLANES + c, 3 * G_W), F32),
            pltpu.VMEM((M_HEADS // 2 * grp, LANES, LANES), F32),
            pltpu.VMEM((M_HEADS // 2 * grp, 1, LANES), F32),
            pltpu.VMEM((M_HEADS // 2 * grp, 1, LANES), F32),
            pltpu.VMEM((G_HEADS // 2 * grp, LANES, LANES), F32),
            pltpu.VMEM((H_HEADS // 2 * grp, LANES, LANES), F32),
        ],
        compiler_params=pltpu.CompilerParams(
            dimension_semantics=("parallel", "arbitrary"), vmem_limit_bytes=VMEM_LIMIT_BYTES),
        name="mixer",
    )(z, prm, cw, lbl, wn, *states)


def _permute_in_proj(w_in, b_in):
    o = 0
    pieces = {}
    for name, width in (("mq", M_W), ("mk", M_W), ("mv", M_W), ("mo", M_W), ("mi", M_HEADS), ("mf", M_HEADS),
                        ("gqkv", 3 * G_W), ("gg", G_W), ("gb", G_HEADS), ("ga", G_HEADS),
                        ("hq", H_W), ("hf", H_W), ("hi", H_W), ("hg", H_W)):
        pieces[name] = (o, o + width)
        o += width
    order = ("mq", "mk", "mv", "mo", "gqkv", "gg", "hq", "hf", "hi", "hg", "mi", "mf", "gb", "ga")

    def perm(a):
        cols = [a[..., pieces[n][0]:pieces[n][1]] for n in order]
        pad = jnp.zeros(a.shape[:-1] + (LANES - GATE_END,), a.dtype)
        return jnp.concatenate(cols + [pad], axis=-1)

    return perm(w_in).astype(BF16), perm(b_in).reshape(DEPTH, 1, Z_W)


def _gate_params(mlstm_f_bias, gdn_A_log, gdn_dt_bias):
    prm = jnp.zeros((DEPTH, SUBLANES, LANES), F32)
    prm = prm.at[:, 0, GATE_MF:GATE_MF + M_HEADS].set(mlstm_f_bias.astype(F32))
    prm = prm.at[:, 0, GATE_GA:GATE_GA + G_HEADS].set(gdn_dt_bias.astype(F32))
    prm = prm.at[:, 1, GATE_GA:GATE_GA + G_HEADS].set(gdn_A_log.astype(F32))
    return prm


def _trunk(x, mod, mod_row0, states, params, c, sb, grp, in_bt, ffn_bt):
    (npre_mix, npost_mix, w_in, b_in, prm, conv_w, lb_logits, out_norm, w_out, npre_ffn, npost_ffn, w_ff1,
     w_ff2) = params
    nb, seq, _ = x.shape
    for l in range(DEPTH):
        z = _inproj_call(x, mod, l, mod_row0, npre_mix[l], w_in[l], b_in[l], *in_bt)
        outs = _mixer_call(l, z.reshape(nb, seq, Z_W), prm[l], conv_w[l], lb_logits, out_norm[l], states, c, sb, grp)
        ymix = outs[0].reshape(nb * seq, D_MIX)
        states = tuple(outs[1:])
        x = _ffn_call(x, ymix, mod, l, mod_row0, w_out[l], w_ff1[l], w_ff2[l], npost_mix[l], npre_ffn[l],
                      npost_ffn[l], *ffn_bt)
    c1, n1, m1, s1, cv1, h1 = states
    return x, (c1, n1, m1.reshape(DEPTH, nb, M_HEADS), s1, cv1, h1)


def kernel(x_prompt, x_sample, state_mlstm_C, state_mlstm_n, state_mlstm_m, state_gdn_S, state_gdn_conv,
           state_hgrn_S, c_prompt, c_sample, w_ada, b_ada, norm_pre_mix, norm_post_mix, w_in, b_in,
           mlstm_f_bias, gdn_conv_w, gdn_A_log, gdn_dt_bias, hgrn_lb_logits, mix_out_norm, w_out,
           norm_pre_ffn, norm_post_ffn, w_ff1, w_ff2):
    bp, seq_p, _ = x_prompt.shape
    bs, seq_s, _ = x_sample.shape

    mod = _ada_call(jnp.concatenate([c_sample, c_prompt], axis=0), w_ada, b_ada)

    w_in_p, b_in_p = _permute_in_proj(w_in, b_in)
    row = lambda a: a.reshape(DEPTH, 1, a.shape[-1]).astype(F32)
    params = (row(norm_pre_mix), row(norm_post_mix), w_in_p, b_in_p,
              _gate_params(mlstm_f_bias, gdn_A_log, gdn_dt_bias), gdn_conv_w.astype(F32),
              hgrn_lb_logits.astype(F32), row(mix_out_norm), w_out.astype(BF16), row(norm_pre_ffn),
              row(norm_post_ffn), w_ff1.astype(BF16), w_ff2.astype(BF16))

    zeros = (jnp.zeros((DEPTH, bp, M_HEADS, HEAD_DIM, HEAD_DIM), F32),
             jnp.zeros((DEPTH, bp, M_HEADS, HEAD_DIM), F32),
             jnp.zeros((DEPTH, bp, 1, M_HEADS), F32),
             jnp.zeros((DEPTH, bp, G_HEADS, HEAD_DIM, HEAD_DIM), F32),
             jnp.zeros((DEPTH, bp, CONV_W - 1, 3 * G_W), F32),
             jnp.zeros((DEPTH, bp, H_HEADS, HEAD_DIM, HEAD_DIM), F32))
    past = (state_mlstm_C.astype(F32), state_mlstm_n.astype(F32),
            state_mlstm_m.astype(F32).reshape(DEPTH, bs, 1, M_HEADS), state_gdn_S.astype(F32),
            state_gdn_conv.astype(F32), state_hgrn_S.astype(F32))

    y_p, st_p = _trunk(x_prompt, mod, bs, zeros, params, PROMPT_CHUNK, HGRN_SUBBLOCK, PROMPT_GROUP,
                       (1, IN_TILE), (1, FFN_TILE))
    y_s, st_s = _trunk(x_sample, mod, 0, past, params, seq_s, min(seq_s, HGRN_SUBBLOCK), SAMPLE_GROUP,
                       (IN_TILE // seq_s, seq_s), (FFN_TILE // seq_s, seq_s))
    return (y_p, y_s) + st_p + st_s
```
